```python
import jax, jax.numpy as jnp
from jax import lax
import numpy as np

D_MODEL = 1024
BATCH = 4
SEQ = 8192
DEPTH = 1
DEC_BATCH = 8
DEC_SEQ = 8192
PAST_LEN = 128

MIX_WIDTH = D_MODEL
ATTN_WIDTH = MIX_WIDTH // 2
POOL_WIDTH = MIX_WIDTH - ATTN_WIDTH
HEAD_DIM = 64
N_HEADS = ATTN_WIDTH // HEAD_DIM
N_KV_HEADS = 2
GQA_GROUP = N_HEADS // N_KV_HEADS
KV_WIDTH = N_KV_HEADS * HEAD_DIM
ROT_PAIRS = HEAD_DIM // 4
ROPE_THETA = 10000.0
GRID_W = 64
Q_BLOCK = 128
POOL_WINDOWS = (2, 4, 8, 16)
N_POOL_GROUPS = len(POOL_WINDOWS)
POOL_GROUP_W = POOL_WIDTH // N_POOL_GROUPS
IN_WIDTH = ATTN_WIDTH + 2 * KV_WIDTH + POOL_WIDTH
N_EXPERTS = 16
CAPACITY_FACTOR = 2
D_FF = 2048
LN_EPS = 1e-5
RMS_EPS = 1e-6
ALPHA = (2.0 * DEPTH) ** 0.25
BETA = (8.0 * DEPTH) ** -0.25

kernel_name = "hybrid_attn_pool_ec_moe_encoder"


def layer_norm(x, g, b):
    xf = x.astype(jnp.float32)
    mu = jnp.mean(xf, axis=-1, keepdims=True)
    var = jnp.mean(jnp.square(xf - mu), axis=-1, keepdims=True)
    return ((xf - mu) * lax.rsqrt(var + LN_EPS) * g.astype(jnp.float32) + b.astype(jnp.float32)).astype(x.dtype)


def rms_norm(x, g):
    xf = x.astype(jnp.float32)
    return (xf * lax.rsqrt(jnp.mean(xf * xf, axis=-1, keepdims=True) + RMS_EPS) * g.astype(jnp.float32)).astype(x.dtype)


def rope_half(u, cos, sin):
    u1, u2 = jnp.split(u, 2, axis=-1)
    return jnp.concatenate([u1 * cos - u2 * sin, u2 * cos + u1 * sin], axis=-1)


def axial_tables(seq_len, dtype):
    rows = seq_len // GRID_W
    row = jnp.repeat(jnp.arange(rows), GRID_W).astype(jnp.float32)
    col = jnp.tile(jnp.arange(GRID_W), rows).astype(jnp.float32)
    inv_freq = ROPE_THETA ** (-jnp.arange(ROT_PAIRS, dtype=jnp.float32) / ROT_PAIRS)
    ang_r = row[:, None] * inv_freq[None, :]
    ang_c = col[:, None] * inv_freq[None, :]
    tabs = [jnp.cos(ang_r), jnp.sin(ang_r), jnp.cos(ang_c), jnp.sin(ang_c)]
    return [t[:, None, :].astype(dtype) for t in tabs]


def axial_rope(x, tabs):
    cos_r, sin_r, cos_c, sin_c = tabs
    xr, xc = jnp.split(x, 2, axis=-1)
    return jnp.concatenate([rope_half(xr, cos_r, sin_r), rope_half(xc, cos_c, sin_c)], axis=-1)


def block_attention(q, k, v):
    B, S = q.shape[0], q.shape[1]
    nblk = S // Q_BLOCK
    scale = HEAD_DIM ** -0.5
    qb = q.reshape(B, nblk, Q_BLOCK, N_KV_HEADS, GQA_GROUP, HEAD_DIM).transpose(1, 0, 2, 3, 4, 5)

    def one_block(qblk):
        s = jnp.einsum('bqhgd,bkhd->bhgqk', qblk, k).astype(jnp.float32) * scale
        p = jax.nn.softmax(s, axis=-1)
        return jnp.einsum('bhgqk,bkhd->bqhgd', p.astype(v.dtype), v)

    ob = lax.map(one_block, qb)
    return ob.transpose(1, 0, 2, 3, 4, 5).reshape(B, S, N_HEADS * HEAD_DIM)


def pool_mixer(u, w_pool, pool_scale):
    B, S, _ = u.shape
    ug = u.reshape(B, S, N_POOL_GROUPS, POOL_GROUP_W)
    cs = jnp.cumsum(ug.astype(jnp.float32), axis=1)
    cs = jnp.concatenate([jnp.zeros((B, 1, N_POOL_GROUPS, POOL_GROUP_W), jnp.float32), cs], axis=1)
    t = jnp.arange(S)
    outs = []
    for g, w in enumerate(POOL_WINDOWS):
        lo = jnp.clip(t - w // 2, 0, S)
        hi = jnp.clip(t - w // 2 + w, 0, S)
        csg = cs[:, :, g, :]
        win_sum = jnp.take(csg, hi, axis=1) - jnp.take(csg, lo, axis=1)
        cnt = (hi - lo).astype(jnp.float32)[None, :, None]
        outs.append(win_sum / cnt - ug[:, :, g, :].astype(jnp.float32))
    m = jnp.stack(outs, axis=2).astype(u.dtype)
    y = jnp.einsum('bsgc,gcd->bsgd', m, w_pool).reshape(B, S, POOL_WIDTH)
    return y * pool_scale


def token_mixer(x, w_in, q_norm_g, k_norm_g, w_pool, pool_scale, w_out):
    B, S, _ = x.shape
    proj = x @ w_in
    q = proj[..., :ATTN_WIDTH].reshape(B, S, N_HEADS, HEAD_DIM)
    k = proj[..., ATTN_WIDTH:ATTN_WIDTH + KV_WIDTH].reshape(B, S, N_KV_HEADS, HEAD_DIM)
    v = proj[..., ATTN_WIDTH + KV_WIDTH:ATTN_WIDTH + 2 * KV_WIDTH].reshape(B, S, N_KV_HEADS, HEAD_DIM)
    u = proj[..., ATTN_WIDTH + 2 * KV_WIDTH:]
    tabs = axial_tables(S, x.dtype)
    q = axial_rope(rms_norm(q, q_norm_g), tabs)
    k = axial_rope(rms_norm(k, k_norm_g), tabs)
    attn = block_attention(q, k, v)
    pool = pool_mixer(u, w_pool, pool_scale)
    return jnp.concatenate([attn, pool], axis=-1) @ w_out


def expert_choice_ffn(x, w_router, w_gate, w_up, w_down):
    B, S, D = x.shape
    n_tok = B * S
    cap = (CAPACITY_FACTOR * n_tok) // N_EXPERTS
    tok = x.reshape(n_tok, D)
    aff = jax.nn.softmax((tok @ w_router).astype(jnp.float32), axis=-1)
    gate, idx = lax.top_k(aff.T, cap)
    xe = jnp.take(tok, idx, axis=0)
    h = jax.nn.silu(jnp.einsum('ecd,edf->ecf', xe, w_gate)) * jnp.einsum('ecd,edf->ecf', xe, w_up)
    ye = jnp.einsum('ecf,efd->ecd', h, w_down) * gate[..., None].astype(x.dtype)
    out = jnp.zeros_like(tok).at[idx.reshape(-1)].add(ye.reshape(-1, D))
    return out.reshape(B, S, D)


def trunk(x, ln_in_g, ln_in_b, w_in, q_norm_g, k_norm_g, w_pool, pool_scale, w_out,
          ln1_g, ln1_b, w_router, w_gate, w_up, w_down, ln2_g, ln2_b):
    x = layer_norm(x, ln_in_g, ln_in_b)
    for l in range(DEPTH):
        mix = token_mixer(x, w_in[l], q_norm_g[l], k_norm_g[l], w_pool[l], pool_scale[l], w_out[l])
        x = layer_norm(ALPHA * x + mix, ln1_g[l], ln1_b[l])
        ff = expert_choice_ffn(x, w_router[l], w_gate[l], w_up[l], w_down[l])
        x = layer_norm(ALPHA * x + ff, ln2_g[l], ln2_b[l])
    return x


def setup_inputs(seed: int = 0) -> dict:
    key = jax.random.key(seed)
    ks = jax.random.split(key, 20)
    f32 = jnp.float32
    nrm = lambda k, shape, s: jax.random.normal(k, shape, f32) * s
    return {
        "x_prompt": jax.random.normal(ks[0], (BATCH, SEQ, D_MODEL), f32),
        "x_sample": jax.random.normal(ks[1], (DEC_BATCH, DEC_SEQ, D_MODEL), f32),
        "ln_in_g": 1.0 + nrm(ks[2], (D_MODEL,), 0.02),
        "ln_in_b": nrm(ks[3], (D_MODEL,), 0.02),
        "w_in": nrm(ks[4], (DEPTH, D_MODEL, IN_WIDTH), D_MODEL ** -0.5),
        "q_norm_g": 1.0 + nrm(ks[5], (DEPTH, HEAD_DIM), 0.02),
        "k_norm_g": 1.0 + nrm(ks[6], (DEPTH, HEAD_DIM), 0.02),
        "w_pool": nrm(ks[7], (DEPTH, N_POOL_GROUPS, POOL_GROUP_W, POOL_GROUP_W), POOL_GROUP_W ** -0.5),
        "pool_scale": 1.0 + nrm(ks[8], (DEPTH, POOL_WIDTH), 0.02),
        "w_out": nrm(ks[9], (DEPTH, MIX_WIDTH, D_MODEL), BETA * MIX_WIDTH ** -0.5),
        "ln1_g": 1.0 + nrm(ks[10], (DEPTH, D_MODEL), 0.02),
        "ln1_b": nrm(ks[11], (DEPTH, D_MODEL), 0.02),
        "w_router": nrm(ks[12], (DEPTH, D_MODEL, N_EXPERTS), D_MODEL ** -0.5),
        "w_gate": nrm(ks[13], (DEPTH, N_EXPERTS, D_MODEL, D_FF), D_MODEL ** -0.5),
        "w_up": nrm(ks[14], (DEPTH, N_EXPERTS, D_MODEL, D_FF), D_MODEL ** -0.5),
        "w_down": nrm(ks[15], (DEPTH, N_EXPERTS, D_FF, D_MODEL), BETA * D_FF ** -0.5),
        "ln2_g": 1.0 + nrm(ks[16], (DEPTH, D_MODEL), 0.02),
        "ln2_b": nrm(ks[17], (DEPTH, D_MODEL), 0.02),
    }


def reference(x_prompt, x_sample, ln_in_g, ln_in_b, w_in, q_norm_g, k_norm_g, w_pool, pool_scale,
              w_out, ln1_g, ln1_b, w_router, w_gate, w_up, w_down, ln2_g, ln2_b):
    y_prompt = trunk(x_prompt, ln_in_g, ln_in_b, w_in, q_norm_g, k_norm_g, w_pool, pool_scale, w_out,
                     ln1_g, ln1_b, w_router, w_gate, w_up, w_down, ln2_g, ln2_b)
    y_sample = trunk(x_sample, ln_in_g, ln_in_b, w_in, q_norm_g, k_norm_g, w_pool, pool_scale, w_out,
                     ln1_g, ln1_b, w_router, w_gate, w_up, w_down, ln2_g, ln2_b)
    return (y_prompt, y_sample)
```

```python
import functools
import math

import jax
import jax.numpy as jnp
import numpy as np
from jax import lax
from jax.experimental import pallas as pl
from jax.experimental.pallas import tpu as pltpu

D_MODEL = 1024
ATTN_WIDTH = 512
POOL_WIDTH = 512
HEAD_DIM = 64
N_HEADS = 8
N_KV_HEADS = 2
GQA_GROUP = N_HEADS // N_KV_HEADS
KV_WIDTH = N_KV_HEADS * HEAD_DIM
ROT_PAIRS = HEAD_DIM // 4
ROPE_THETA = 10000.0
GRID_W = 64
POOL_WINDOWS = (2, 4, 8, 16)
POOL_GROUP_W = 128
N_EXPERTS = 16
CAPACITY_FACTOR = 2
D_FF = 2048
LN_EPS = 1e-5
RMS_EPS = 1e-6
ALPHA = 2.0 ** 0.25

LANES = 128
V7X_VMEM_LIMIT = 56 * 1024 * 1024

HALO = 16
V_ROWS = 80
SLOT_WIN = 64
TOK_TILE = 256


def _ln(x, g, b):
    mu = jnp.mean(x, axis=-1, keepdims=True)
    xc = x - mu
    var = jnp.mean(xc * xc, axis=-1, keepdims=True)
    return xc * lax.rsqrt(var + LN_EPS) * g + b


def _cp(sem, vmem=V7X_VMEM_LIMIT):
    return pltpu.CompilerParams(dimension_semantics=sem, vmem_limit_bytes=vmem)


def _in_proj_kernel(x_ref, g_ref, b_ref, wt_ref, wu_ref, ct_ref, st_ref, gq_ref, gk_ref,
                    q_ref, k_ref, v_ref, u_ref, *, q_scale):
    x0 = _ln(x_ref[0], g_ref[...], b_ref[...])
    xb = x0.astype(jnp.bfloat16)
    pt = lax.dot_general(wt_ref[...], xb, (((1,), (1,)), ((), ())),
                         preferred_element_type=jnp.float32)
    u_ref[0] = jnp.dot(xb, wu_ref[...], preferred_element_type=jnp.float32)
    ct = ct_ref[...]
    st = st_ref[...]

    def norm_rope(xh, gcol):
        ms = jnp.mean(xh * xh, axis=0, keepdims=True)
        xn = xh * lax.rsqrt(ms + RMS_EPS) * gcol
        partner = jnp.concatenate([xn[16:32], xn[0:16], xn[48:64], xn[32:48]], axis=0)
        return xn * ct + partner * st

    gq = gq_ref[...]
    gk = gk_ref[...]
    for h in range(N_HEADS):
        qh = norm_rope(pt[h * HEAD_DIM:(h + 1) * HEAD_DIM], gq) * q_scale
        q_ref[0, h * HEAD_DIM:(h + 1) * HEAD_DIM, :] = qh.astype(jnp.bfloat16)
    kt = jnp.concatenate(
        [norm_rope(pt[ATTN_WIDTH + h * HEAD_DIM:ATTN_WIDTH + (h + 1) * HEAD_DIM], gk)
         for h in range(N_KV_HEADS)], axis=0)
    k_ref[0] = jnp.transpose(kt).astype(jnp.bfloat16)
    ones = jnp.ones((V_ROWS - HEAD_DIM, pt.shape[1]), jnp.bfloat16)
    for h in range(N_KV_HEADS):
        r0 = ATTN_WIDTH + KV_WIDTH + h * HEAD_DIM
        v_ref[0, h, 0:HEAD_DIM, :] = pt[r0:r0 + HEAD_DIM].astype(jnp.bfloat16)
        v_ref[0, h, HEAD_DIM:V_ROWS, :] = ones


def _in_proj(x, ln_g, ln_b, wt, wu, ct, st, gq, gk, tile):
    B, S, _ = x.shape
    q_scale = (HEAD_DIM ** -0.5) * math.log2(math.e)
    const = lambda shape: pl.BlockSpec(shape, lambda b, i: tuple(0 for _ in shape))
    return pl.pallas_call(
        functools.partial(_in_proj_kernel, q_scale=q_scale),
        grid=(B, S // tile),
        in_specs=[
            pl.BlockSpec((1, tile, D_MODEL), lambda b, i: (b, i, 0)),
            const((1, D_MODEL)), const((1, D_MODEL)),
            const((ATTN_WIDTH + 2 * KV_WIDTH, D_MODEL)),
            const((D_MODEL, POOL_WIDTH)),
            pl.BlockSpec((HEAD_DIM, tile), lambda b, i: (0, i)),
            pl.BlockSpec((HEAD_DIM, tile), lambda b, i: (0, i)),
            const((HEAD_DIM, 1)), const((HEAD_DIM, 1)),
        ],
        out_specs=[
            pl.BlockSpec((1, ATTN_WIDTH, tile), lambda b, i: (b, 0, i)),
            pl.BlockSpec((1, tile, KV_WIDTH), lambda b, i: (b, i, 0)),
            pl.BlockSpec((1, N_KV_HEADS, V_ROWS, tile), lambda b, i: (b, 0, 0, i)),
            pl.BlockSpec((1, tile, POOL_WIDTH), lambda b, i: (b, i, 0)),
        ],
        out_shape=[
            jax.ShapeDtypeStruct((B, ATTN_WIDTH, S), jnp.bfloat16),
            jax.ShapeDtypeStruct((B, S, KV_WIDTH), jnp.bfloat16),
            jax.ShapeDtypeStruct((B, N_KV_HEADS, V_ROWS, S), jnp.bfloat16),
            jax.ShapeDtypeStruct((B, S, POOL_WIDTH), jnp.float32),
        ],
        compiler_params=_cp(("parallel", "parallel")),
        name="in_proj",
    )(x, ln_g, ln_b, wt, wu, ct, st, gq, gk)


def _attn_kernel(q_ref, k_ref, v_ref, o_ref, qp_ref, m_ref, acc_ref, *, tk):
    S = k_ref.shape[1]
    tq = q_ref.shape[2]
    zeros = jnp.zeros((HEAD_DIM, tq), jnp.bfloat16)
    for h in range(N_HEADS):
        qh = q_ref[0, h * HEAD_DIM:(h + 1) * HEAD_DIM, :]
        qp_ref[h] = jnp.concatenate([qh, zeros] if h < GQA_GROUP else [zeros, qh], axis=0)
    m_ref[...] = jnp.full(m_ref.shape, -jnp.inf, jnp.float32)
    acc_ref[...] = jnp.zeros(acc_ref.shape, jnp.float32)

    def chunk(c, carry):
        c0 = pl.multiple_of(c * tk, tk)
        kc = k_ref[0, pl.ds(c0, tk), :]
        for hk in range(N_KV_HEADS):
            vc = v_ref[0, hk, :, pl.ds(c0, tk)]
            for g in range(GQA_GROUP):
                h = hk * GQA_GROUP + g
                s = jnp.dot(kc, qp_ref[h], preferred_element_type=jnp.float32)
                m_old = m_ref[h]
                m_new = jnp.maximum(m_old, jnp.max(s, axis=0, keepdims=True))
                alpha = jnp.exp2(m_old - m_new)
                p = jnp.exp2(s - m_new).astype(jnp.bfloat16)
                acc_ref[h] = alpha * acc_ref[h] + jnp.dot(vc, p, preferred_element_type=jnp.float32)
                m_ref[h] = m_new
        return carry

    lax.fori_loop(0, S // tk, chunk, 0)
    for h in range(N_HEADS):
        a = acc_ref[h]
        o_ref[0, h * HEAD_DIM:(h + 1) * HEAD_DIM, :] = (
            a[0:HEAD_DIM] / a[HEAD_DIM:HEAD_DIM + 1]).astype(jnp.bfloat16)


def _attention(qt, k, vt, tq, tk):
    B, _, S = qt.shape
    return pl.pallas_call(
        functools.partial(_attn_kernel, tk=tk),
        grid=(B, S // tq),
        in_specs=[
            pl.BlockSpec((1, ATTN_WIDTH, tq), lambda b, i: (b, 0, i)),
            pl.BlockSpec((1, S, KV_WIDTH), lambda b, i: (b, 0, 0)),
            pl.BlockSpec((1, N_KV_HEADS, V_ROWS, S), lambda b, i: (b, 0, 0, 0)),
        ],
        out_specs=pl.BlockSpec((1, ATTN_WIDTH, tq), lambda b, i: (b, 0, i)),
        out_shape=jax.ShapeDtypeStruct((B, ATTN_WIDTH, S), jnp.bfloat16),
        scratch_shapes=[
            pltpu.VMEM((N_HEADS, KV_WIDTH, tq), jnp.bfloat16),
            pltpu.VMEM((N_HEADS, 1, tq), jnp.float32),
            pltpu.VMEM((N_HEADS, V_ROWS, tq), jnp.float32),
        ],
        compiler_params=_cp(("parallel", "parallel")),
        name="attention",
    )(qt, k, vt)


def _mix_kernel(x_ref, at_ref, uc_ref, up_ref, un_ref, band_ref, wp_ref, ps_ref, woa_ref, wop_ref,
                lg_ref, lb_ref, g1_ref, b1_ref, wr_ref, h_ref, aff_ref, *, seq_len):
    i = pl.program_id(1)
    T = uc_ref.shape[1]
    n_tiles = seq_len // T
    up = jnp.where(i > 0, up_ref[0], 0.0)
    un = jnp.where(i < n_tiles - 1, un_ref[0], 0.0)
    uc = uc_ref[0]
    uext = jnp.concatenate([up, uc, un], axis=0)
    uhi = uext.astype(jnp.bfloat16)
    ulo = (uext - uhi.astype(jnp.float32)).astype(jnp.bfloat16)
    tg = (i * T + lax.broadcasted_iota(jnp.int32, (T, 1), 0))
    pooled = []
    for g, w in enumerate(POOL_WINDOWS):
        cols = slice(g * POOL_GROUP_W, (g + 1) * POOL_GROUP_W)
        band = band_ref[g]
        win = (jnp.dot(band, uhi[:, cols], preferred_element_type=jnp.float32)
               + jnp.dot(band, ulo[:, cols], preferred_element_type=jnp.float32))
        lo = jnp.maximum(tg - w // 2, 0)
        hi = jnp.minimum(tg - w // 2 + w, seq_len)
        cnt = (hi - lo).astype(jnp.float32)
        m = win / cnt - uc[:, cols]
        y = jnp.dot(m.astype(jnp.bfloat16), wp_ref[g], preferred_element_type=jnp.float32)
        pooled.append(y * ps_ref[:, cols])
    pool = jnp.concatenate(pooled, axis=1).astype(jnp.bfloat16)
    mix = lax.dot_general(at_ref[0], woa_ref[...], (((0,), (0,)), ((), ())),
                          preferred_element_type=jnp.float32)
    mix = mix + jnp.dot(pool, wop_ref[...], preferred_element_type=jnp.float32)
    x0 = _ln(x_ref[0], lg_ref[...], lb_ref[...])
    h = _ln(ALPHA * x0 + mix, g1_ref[...], b1_ref[...])
    h_ref[0] = h
    logits = lax.dot_general(wr_ref[...], h.astype(jnp.bfloat16), (((1,), (1,)), ((), ())),
                             preferred_element_type=jnp.float32)
    e = jnp.exp(logits - jnp.max(logits, axis=0, keepdims=True))
    aff_ref[...] = e / jnp.sum(e, axis=0, keepdims=True)


def _mix_out(x, attn_t, u, band, wp, ps, woa, wop, lg, lb, g1, b1, wr_t, tile):
    B, S, _ = x.shape
    nt = S // tile
    hb = tile // HALO
    const = lambda shape: pl.BlockSpec(shape, lambda b, i: tuple(0 for _ in shape))
    return pl.pallas_call(
        functools.partial(_mix_kernel, seq_len=S),
        grid=(B, nt),
        in_specs=[
            pl.BlockSpec((1, tile, D_MODEL), lambda b, i: (b, i, 0)),
            pl.BlockSpec((1, ATTN_WIDTH, tile), lambda b, i: (b, 0, i)),
            pl.BlockSpec((1, tile, POOL_WIDTH), lambda b, i: (b, i, 0)),
            pl.BlockSpec((1, HALO, POOL_WIDTH), lambda b, i: (b, jnp.maximum(i * hb - 1, 0), 0)),
            pl.BlockSpec((1, HALO, POOL_WIDTH),
                         lambda b, i: (b, jnp.minimum((i + 1) * hb, S // HALO - 1), 0)),
            const(band.shape), const(wp.shape), const((1, POOL_WIDTH)),
            const((ATTN_WIDTH, D_MODEL)), const((POOL_WIDTH, D_MODEL)),
            const((1, D_MODEL)), const((1, D_MODEL)), const((1, D_MODEL)), const((1, D_MODEL)),
            const((N_EXPERTS, D_MODEL)),
        ],
        out_specs=[
            pl.BlockSpec((1, tile, D_MODEL), lambda b, i: (b, i, 0)),
            pl.BlockSpec((N_EXPERTS, tile), lambda b, i: (0, b * nt + i)),
        ],
        out_shape=[
            jax.ShapeDtypeStruct((B, S, D_MODEL), jnp.float32),
            jax.ShapeDtypeStruct((N_EXPERTS, B * S), jnp.float32),
        ],
        compiler_params=_cp(("parallel", "parallel")),
        name="mix_out",
    )(x, attn_t, u, u, u, band, wp, ps, woa, wop, lg, lb, g1, b1, wr_t)


def _route_kernel(aff_ref, tri_ref, ones_ref, low_ref, pos_ref, roff_ref, thr_ref, *, cap):
    n_e, R, _ = aff_ref.shape
    tri = tri_ref[...]
    ones = ones_ref[...]
    low = low_ref[...]

    def excl_prefix(flag):
        fb = flag.astype(jnp.bfloat16)
        incl = jnp.dot(fb, tri, preferred_element_type=jnp.float32)
        tot = jnp.dot(fb, ones, preferred_element_type=jnp.float32)
        before = jnp.dot(low, tot.astype(jnp.bfloat16), preferred_element_type=jnp.float32)
        return before + incl - flag, before

    for e in range(n_e):
        thr_ref[e] = 0

    def bit_step(it, carry):
        bit = 30 - it
        for e in range(n_e):
            keys = pltpu.bitcast(aff_ref[e], jnp.int32)
            cand = thr_ref[e] | lax.shift_left(jnp.int32(1), bit)
            cnt = jnp.sum(jnp.where(keys >= cand, 1.0, 0.0))
            thr_ref[e] = jnp.where(cnt >= float(cap), cand, thr_ref[e])
        return carry

    lax.fori_loop(0, 31, bit_step, 0)

    for e in range(n_e):
        keys = pltpu.bitcast(aff_ref[e], jnp.int32)
        thr = thr_ref[e]
        gt = jnp.where(keys > thr, 1.0, 0.0)
        eq = jnp.where(keys == thr, 1.0, 0.0)
        need = float(cap) - jnp.sum(gt)
        eq_rank, _ = excl_prefix(eq)
        sel = gt + eq * jnp.where(eq_rank < need, 1.0, 0.0)
        pos, before = excl_prefix(sel)
        pos_ref[e] = jnp.where(sel > 0.5, pos, -1.0).astype(jnp.int32)
        roff_ref[e] = before.astype(jnp.int32)


def _route(aff3, cap):
    n_e, R, _ = aff3.shape
    tri = jnp.asarray(np.triu(np.ones((LANES, LANES), np.float32)), jnp.bfloat16)
    ones = jnp.ones((LANES, LANES), jnp.bfloat16)
    low = jnp.asarray(np.tril(np.ones((R, R), np.float32), -1), jnp.bfloat16)
    return pl.pallas_call(
        functools.partial(_route_kernel, cap=cap),
        out_shape=[jax.ShapeDtypeStruct((n_e, R, LANES), jnp.int32),
                   jax.ShapeDtypeStruct((n_e, R, LANES), jnp.int32)],
        scratch_shapes=[pltpu.SMEM((n_e,), jnp.int32)],
        compiler_params=pltpu.CompilerParams(vmem_limit_bytes=V7X_VMEM_LIMIT),
        name="route",
    )(aff3, tri, ones, low)


def _window_onehot(pos_ref, off_ref, j, p, nt1, tile, value_rows=None):
    rows = lax.broadcasted_iota(jnp.int32, (SLOT_WIN, tile), 0)
    blocks, bases = [], []
    for e in range(N_EXPERTS):
        a = off_ref[e * nt1 + j]
        base = (lax.shift_right_logical(a, int(math.log2(SLOT_WIN))) + p) * SLOT_WIN
        rel = pos_ref[e:e + 1, :] - base
        hit = rows == rel
        if value_rows is None:
            blocks.append(jnp.where(hit, 1.0, 0.0).astype(jnp.bfloat16))
        else:
            blocks.append(jnp.where(hit, value_rows[e:e + 1, :], 0.0).astype(jnp.bfloat16))
        bases.append(base)
    return jnp.concatenate(blocks, axis=0), bases


def _gather_kernel(off_ref, np_ref, pos_ref, h_ref, xe_ref, acc_ref, stage_ref, pend_ref, sem,
                   *, nt, cap):
    j = pl.program_id(0)
    nt1 = nt + 1
    tile = h_ref.shape[0]

    @pl.when(j == 0)
    def _():
        acc_ref[...] = jnp.zeros(acc_ref.shape, jnp.float32)
        for e in range(N_EXPERTS):
            pend_ref[e] = 0

    hb = h_ref[...].astype(jnp.bfloat16)

    def flush_copy(e, base):
        return pltpu.make_async_copy(stage_ref.at[e], xe_ref.at[e, pl.ds(base, SLOT_WIN)], sem.at[e])

    def one_pass(p, carry):
        onehot, bases = _window_onehot(pos_ref, off_ref, j, p, nt1, tile)
        contrib = jnp.dot(onehot, hb, preferred_element_type=jnp.float32)
        for e in range(N_EXPERTS):
            acc_ref[e] += contrib[e * SLOT_WIN:(e + 1) * SLOT_WIN]
            done = off_ref[e * nt1 + j + 1] >= bases[e] + SLOT_WIN

            @pl.when(done)
            def _(e=e):
                @pl.when(pend_ref[e] == 1)
                def _():
                    flush_copy(e, 0).wait()
                stage_ref[e] = acc_ref[e].astype(jnp.bfloat16)
                flush_copy(e, pl.multiple_of(bases[e], SLOT_WIN)).start()
                pend_ref[e] = 1
                acc_ref[e] = jnp.zeros((SLOT_WIN, D_MODEL), jnp.float32)
        return carry

    lax.fori_loop(0, np_ref[j], one_pass, 0)

    @pl.when(j == nt - 1)
    def _():
        for e in range(N_EXPERTS):
            @pl.when(pend_ref[e] == 1)
            def _(e=e):
                flush_copy(e, 0).wait()
                pend_ref[e] = 0


def _gather(off, npass, pos, h, cap):
    N = h.shape[0]
    nt = N // TOK_TILE
    return pl.pallas_call(
        functools.partial(_gather_kernel, nt=nt, cap=cap),
        grid_spec=pltpu.PrefetchScalarGridSpec(
            num_scalar_prefetch=2,
            grid=(nt,),
            in_specs=[
                pl.BlockSpec((N_EXPERTS, TOK_TILE), lambda j, *_: (0, j)),
                pl.BlockSpec((TOK_TILE, D_MODEL), lambda j, *_: (j, 0)),
            ],
            out_specs=pl.BlockSpec(memory_space=pl.ANY),
            scratch_shapes=[
                pltpu.VMEM((N_EXPERTS, SLOT_WIN, D_MODEL), jnp.float32),
                pltpu.VMEM((N_EXPERTS, SLOT_WIN, D_MODEL), jnp.bfloat16),
                pltpu.SMEM((N_EXPERTS,), jnp.int32),
                pltpu.SemaphoreType.DMA((N_EXPERTS,)),
            ],
        ),
        out_shape=jax.ShapeDtypeStruct((N_EXPERTS, cap, D_MODEL), jnp.bfloat16),
        compiler_params=_cp(("arbitrary",)),
        name="gather",
    )(off, npass, pos, h)


def _ffn_kernel(x_ref, wg_ref, wu_ref, wd_ref, y_ref):
    x = x_ref[0]
    half = D_FF // 2
    acc = None
    for c in range(2):
        cols = slice(c * half, (c + 1) * half)
        g = jnp.dot(x, wg_ref[0, :, cols], preferred_element_type=jnp.float32)
        u = jnp.dot(x, wu_ref[0, :, cols], preferred_element_type=jnp.float32)
        hid = (g * jax.nn.sigmoid(g) * u).astype(jnp.bfloat16)
        part = jnp.dot(hid, wd_ref[0, cols, :], preferred_element_type=jnp.float32)
        acc = part if acc is None else acc + part
    y_ref[0] = acc.astype(jnp.bfloat16)


def _ffn(xe, wg, wu, wd, tile):
    n_e, cap, _ = xe.shape
    return pl.pallas_call(
        _ffn_kernel,
        grid=(n_e, cap // tile),
        in_specs=[
            pl.BlockSpec((1, tile, D_MODEL), lambda e, s: (e, s, 0)),
            pl.BlockSpec((1, D_MODEL, D_FF), lambda e, s: (e, 0, 0)),
            pl.BlockSpec((1, D_MODEL, D_FF), lambda e, s: (e, 0, 0)),
            pl.BlockSpec((1, D_FF, D_MODEL), lambda e, s: (e, 0, 0)),
        ],
        out_specs=pl.BlockSpec((1, tile, D_MODEL), lambda e, s: (e, s, 0)),
        out_shape=jax.ShapeDtypeStruct((n_e, cap, D_MODEL), jnp.bfloat16),
        compiler_params=_cp(("parallel", "arbitrary")),
        name="ffn",
    )(xe, wg, wu, wd)


def _combine_kernel(off_ref, np_ref, pos_ref, aff_ref, h_ref, g2_ref, b2_ref, ye_ref, y_ref,
                    win_ref, slot_ref, sem, *, nt, cap):
    j = pl.program_id(0)
    nt1 = nt + 1
    tile = h_ref.shape[0]
    log_w = int(math.log2(SLOT_WIN))

    def fetch_copy(e, base, slot):
        return pltpu.make_async_copy(ye_ref.at[e, pl.ds(base, SLOT_WIN)], win_ref.at[slot, e],
                                     sem.at[slot])

    def start_fetch(jj, p, slot):
        for e in range(N_EXPERTS):
            a = off_ref[e * nt1 + jj]
            base = (lax.shift_right_logical(a, log_w) + p) * SLOT_WIN
            base = jnp.minimum(base, cap - SLOT_WIN)
            fetch_copy(e, pl.multiple_of(base, SLOT_WIN), slot).start()

    def wait_fetch(slot):
        for e in range(N_EXPERTS):
            fetch_copy(e, 0, slot).wait()

    @pl.when(j == 0)
    def _():
        slot_ref[0] = 0
        start_fetch(0, 0, 0)

    n_pass = jnp.maximum(np_ref[j], 1)
    gates = aff_ref[...]

    def one_pass(p, ff):
        slot = slot_ref[0]
        wait_fetch(slot)

        @pl.when(p + 1 < n_pass)
        def _():
            start_fetch(j, p + 1, 1 - slot)

        @pl.when(jnp.logical_and(p + 1 >= n_pass, j + 1 < nt))
        def _():
            start_fetch(j + 1, 0, 1 - slot)

        onehot, _ = _window_onehot(pos_ref, off_ref, j, p, nt1, tile, value_rows=gates)
        rows = win_ref[slot].reshape(N_EXPERTS * SLOT_WIN, D_MODEL)
        ff = ff + lax.dot_general(onehot, rows, (((0,), (0,)), ((), ())),
                                  preferred_element_type=jnp.float32)
        slot_ref[0] = 1 - slot
        return ff

    ff = lax.fori_loop(0, n_pass, one_pass, jnp.zeros((tile, D_MODEL), jnp.float32))
    y_ref[...] = _ln(ALPHA * h_ref[...] + ff, g2_ref[...], b2_ref[...])


def _combine(off, npass, pos, aff, h, g2, b2, ye, cap):
    N = h.shape[0]
    nt = N // TOK_TILE
    return pl.pallas_call(
        functools.partial(_combine_kernel, nt=nt, cap=cap),
        grid_spec=pltpu.PrefetchScalarGridSpec(
            num_scalar_prefetch=2,
            grid=(nt,),
            in_specs=[
                pl.BlockSpec((N_EXPERTS, TOK_TILE), lambda j, *_: (0, j)),
                pl.BlockSpec((N_EXPERTS, TOK_TILE), lambda j, *_: (0, j)),
                pl.BlockSpec((TOK_TILE, D_MODEL), lambda j, *_: (j, 0)),
                pl.BlockSpec((1, D_MODEL), lambda j, *_: (0, 0)),
                pl.BlockSpec((1, D_MODEL), lambda j, *_: (0, 0)),
                pl.BlockSpec(memory_space=pl.ANY),
            ],
            out_specs=pl.BlockSpec((TOK_TILE, D_MODEL), lambda j, *_: (j, 0)),
            scratch_shapes=[
                pltpu.VMEM((2, N_EXPERTS, SLOT_WIN, D_MODEL), jnp.bfloat16),
                pltpu.SMEM((1,), jnp.int32),
                pltpu.SemaphoreType.DMA((2,)),
            ],
        ),
        out_shape=jax.ShapeDtypeStruct((N, D_MODEL), jnp.float32),
        compiler_params=_cp(("arbitrary",)),
        name="combine",
    )(off, npass, pos, aff, h, g2, b2, ye)


def _rope_tables(seq_len):
    rows = seq_len // GRID_W
    row = jnp.repeat(jnp.arange(rows), GRID_W).astype(jnp.float32)
    col = jnp.tile(jnp.arange(GRID_W), rows).astype(jnp.float32)
    inv_freq = ROPE_THETA ** (-jnp.arange(ROT_PAIRS, dtype=jnp.float32) / ROT_PAIRS)
    ang_r = inv_freq[:, None] * row[None, :]
    ang_c = inv_freq[:, None] * col[None, :]
    ct = jnp.concatenate([jnp.cos(ang_r), jnp.cos(ang_r), jnp.cos(ang_c), jnp.cos(ang_c)], axis=0)
    st = jnp.concatenate([-jnp.sin(ang_r), jnp.sin(ang_r), -jnp.sin(ang_c), jnp.sin(ang_c)], axis=0)
    return ct, st


def _pool_bands(tile):
    t = np.arange(tile)[:, None]
    j = np.arange(tile + 2 * HALO)[None, :] - HALO
    bands = [((j - t >= -(w // 2)) & (j - t < w - w // 2)).astype(np.float32) for w in POOL_WINDOWS]
    return jnp.asarray(np.stack(bands), jnp.bfloat16)


def _trunk(x, prm, *, proj_tile, attn_tq, attn_tk, mix_tile, ffn_tile):
    B, S, _ = x.shape
    N = B * S
    cap = (CAPACITY_FACTOR * N) // N_EXPERTS
    assert S % proj_tile == 0 and S % attn_tq == 0 and S % attn_tk == 0 and S % mix_tile == 0
    assert N % TOK_TILE == 0 and cap % SLOT_WIN == 0 and cap % ffn_tile == 0 and TOK_TILE % LANES == 0

    ct, st = _rope_tables(S)
    qt, k, vt, u = _in_proj(x, prm["ln_in_g"], prm["ln_in_b"], prm["wt"], prm["wu"], ct, st,
                            prm["gq"], prm["gk"], proj_tile)
    attn_t = _attention(qt, k, vt, attn_tq, attn_tk)
    h, aff = _mix_out(x, attn_t, u, _pool_bands(mix_tile), prm["w_pool"], prm["pool_scale"],
                      prm["wo_attn"], prm["wo_pool"], prm["ln_in_g"], prm["ln_in_b"],
                      prm["ln1_g"], prm["ln1_b"], prm["wr_t"], mix_tile)
    h = h.reshape(N, D_MODEL)

    pos3, roff3 = _route(aff.reshape(N_EXPERTS, N // LANES, LANES), cap)
    pos = pos3.reshape(N_EXPERTS, N)
    nt = N // TOK_TILE
    off = jnp.concatenate([roff3[:, ::TOK_TILE // LANES, 0],
                           jnp.full((N_EXPERTS, 1), cap, jnp.int32)], axis=1)
    a, b = off[:, :-1], off[:, 1:]
    wins = jnp.where(b > a, (b - 1) // SLOT_WIN - a // SLOT_WIN + 1, 0)
    npass = jnp.max(wins, axis=0).astype(jnp.int32)
    off = off.reshape(-1)

    xe = _gather(off, npass, pos, h, cap)
    ye = _ffn(xe, prm["w_gate"], prm["w_up"], prm["w_down"], ffn_tile)
    y = _combine(off, npass, pos, aff, h, prm["ln2_g"], prm["ln2_b"], ye, cap)
    return y.reshape(B, S, D_MODEL)


def _prepare(ln_in_g, ln_in_b, w_in, q_norm_g, k_norm_g, w_pool, pool_scale, w_out, ln1_g, ln1_b,
             w_router, w_gate, w_up, w_down, ln2_g, ln2_b):
    bf = jnp.bfloat16
    row = lambda v: v.reshape(1, -1).astype(jnp.float32)
    qkv = ATTN_WIDTH + 2 * KV_WIDTH
    return {
        "ln_in_g": row(ln_in_g), "ln_in_b": row(ln_in_b),
        "wt": jnp.transpose(w_in[0][:, :qkv]).astype(bf),
        "wu": w_in[0][:, qkv:].astype(bf),
        "gq": q_norm_g[0].reshape(HEAD_DIM, 1), "gk": k_norm_g[0].reshape(HEAD_DIM, 1),
        "w_pool": w_pool[0].astype(bf), "pool_scale": row(pool_scale[0]),
        "wo_attn": w_out[0][:ATTN_WIDTH].astype(bf), "wo_pool": w_out[0][ATTN_WIDTH:].astype(bf),
        "ln1_g": row(ln1_g[0]), "ln1_b": row(ln1_b[0]),
        "wr_t": jnp.transpose(w_router[0]).astype(bf),
        "w_gate": w_gate[0].astype(bf), "w_up": w_up[0].astype(bf), "w_down": w_down[0].astype(bf),
        "ln2_g": row(ln2_g[0]), "ln2_b": row(ln2_b[0]),
    }


def kernel(x_prompt, x_sample, ln_in_g, ln_in_b, w_in, q_norm_g, k_norm_g, w_pool, pool_scale, w_out,
           ln1_g, ln1_b, w_router, w_gate, w_up, w_down, ln2_g, ln2_b):
    prm = _prepare(ln_in_g, ln_in_b, w_in, q_norm_g, k_norm_g, w_pool, pool_scale, w_out, ln1_g, ln1_b,
                   w_router, w_gate, w_up, w_down, ln2_g, ln2_b)
    tiles = dict(proj_tile=512, attn_tq=256, attn_tk=512, mix_tile=256, ffn_tile=512)
    return (_trunk(x_prompt, prm, **tiles), _trunk(x_sample, prm, **tiles))
```

```python
import functools
import math

import jax
import jax.numpy as jnp
import numpy as np
from jax import lax
from jax.experimental import pallas as pl
from jax.experimental.pallas import tpu as pltpu

D_MODEL = 1024
ATTN_WIDTH = 512
POOL_WIDTH = 512
HEAD_DIM = 64
N_HEADS = 8
N_KV_HEADS = 2
GQA_GROUP = N_HEADS // N_KV_HEADS
KV_WIDTH = N_KV_HEADS * HEAD_DIM
ROT_PAIRS = HEAD_DIM // 4
ROPE_THETA = 10000.0
GRID_W = 64
POOL_WINDOWS = (2, 4, 8, 16)
POOL_GROUP_W = 128
N_EXPERTS = 16
CAPACITY_FACTOR = 2
D_FF = 2048
LN_EPS = 1e-5
RMS_EPS = 1e-6
ALPHA = 2.0 ** 0.25

LANES = 128
BF16_SUBLANES = 16
V7X_VMEM_LIMIT = 56 * 1024 * 1024

HALO = 16
V_ROWS = 80
SLOT_WIN = 64
TOK_TILE = 256


def _ln(x, g, b):
    mu = jnp.mean(x, axis=-1, keepdims=True)
    xc = x - mu
    var = jnp.mean(xc * xc, axis=-1, keepdims=True)
    return xc * lax.rsqrt(var + LN_EPS) * g + b


def _cp(sem, vmem=V7X_VMEM_LIMIT):
    return pltpu.CompilerParams(dimension_semantics=sem, vmem_limit_bytes=vmem)


def _in_proj_kernel(x_ref, g_ref, b_ref, wt_ref, wu_ref, ct_ref, st_ref, gq_ref, gk_ref,
                    q_ref, k_ref, v_ref, u_ref, *, q_scale):
    x0 = _ln(x_ref[0], g_ref[...], b_ref[...])
    xb = x0.astype(jnp.bfloat16)
    pt = lax.dot_general(wt_ref[...], xb, (((1,), (1,)), ((), ())),
                         preferred_element_type=jnp.float32)
    u_ref[0] = jnp.dot(xb, wu_ref[...], preferred_element_type=jnp.float32)
    ct = ct_ref[...]
    st = st_ref[...]

    def norm_rope(xh, gcol):
        ms = jnp.mean(xh * xh, axis=0, keepdims=True)
        xn = xh * lax.rsqrt(ms + RMS_EPS) * gcol
        partner = jnp.concatenate([xn[16:32], xn[0:16], xn[48:64], xn[32:48]], axis=0)
        return xn * ct + partner * st

    gq = gq_ref[...]
    gk = gk_ref[...]
    for h in range(N_HEADS):
        qh = norm_rope(pt[h * HEAD_DIM:(h + 1) * HEAD_DIM], gq) * q_scale
        q_ref[0, h * HEAD_DIM:(h + 1) * HEAD_DIM, :] = qh.astype(jnp.bfloat16)
    kt = jnp.concatenate(
        [norm_rope(pt[ATTN_WIDTH + h * HEAD_DIM:ATTN_WIDTH + (h + 1) * HEAD_DIM], gk)
         for h in range(N_KV_HEADS)], axis=0)
    k_ref[0] = jnp.transpose(kt).astype(jnp.bfloat16)
    ones = jnp.ones((V_ROWS - HEAD_DIM, pt.shape[1]), jnp.bfloat16)
    for h in range(N_KV_HEADS):
        r0 = ATTN_WIDTH + KV_WIDTH + h * HEAD_DIM
        v_ref[0, h, 0:HEAD_DIM, :] = pt[r0:r0 + HEAD_DIM].astype(jnp.bfloat16)
        v_ref[0, h, HEAD_DIM:V_ROWS, :] = ones


def _in_proj(x, ln_g, ln_b, wt, wu, ct, st, gq, gk, tile):
    B, S, _ = x.shape
    q_scale = (HEAD_DIM ** -0.5) * math.log2(math.e)
    const = lambda shape: pl.BlockSpec(shape, lambda b, i: tuple(0 for _ in shape))
    return pl.pallas_call(
        functools.partial(_in_proj_kernel, q_scale=q_scale),
        grid=(B, S // tile),
        in_specs=[
            pl.BlockSpec((1, tile, D_MODEL), lambda b, i: (b, i, 0)),
            const((1, D_MODEL)), const((1, D_MODEL)),
            const((ATTN_WIDTH + 2 * KV_WIDTH, D_MODEL)),
            const((D_MODEL, POOL_WIDTH)),
            pl.BlockSpec((HEAD_DIM, tile), lambda b, i: (0, i)),
            pl.BlockSpec((HEAD_DIM, tile), lambda b, i: (0, i)),
            const((HEAD_DIM, 1)), const((HEAD_DIM, 1)),
        ],
        out_specs=[
            pl.BlockSpec((1, ATTN_WIDTH, tile), lambda b, i: (b, 0, i)),
            pl.BlockSpec((1, tile, KV_WIDTH), lambda b, i: (b, i, 0)),
            pl.BlockSpec((1, N_KV_HEADS, V_ROWS, tile), lambda b, i: (b, 0, 0, i)),
            pl.BlockSpec((1, tile, POOL_WIDTH), lambda b, i: (b, i, 0)),
        ],
        out_shape=[
            jax.ShapeDtypeStruct((B, ATTN_WIDTH, S), jnp.bfloat16),
            jax.ShapeDtypeStruct((B, S, KV_WIDTH), jnp.bfloat16),
            jax.ShapeDtypeStruct((B, N_KV_HEADS, V_ROWS, S), jnp.bfloat16),
            jax.ShapeDtypeStruct((B, S, POOL_WIDTH), jnp.float32),
        ],
        compiler_params=_cp(("parallel", "parallel")),
        name="in_proj",
    )(x, ln_g, ln_b, wt, wu, ct, st, gq, gk)


def _attn_kernel(q_ref, k_ref, v_ref, o_ref, qp_ref, m_ref, acc_ref, s_ref, *, tk):
    S = k_ref.shape[1]
    tq = q_ref.shape[2]
    n_chunks = S // tk
    zeros = jnp.zeros((HEAD_DIM, tq), jnp.bfloat16)
    for h in range(N_HEADS):
        qh = q_ref[0, h * HEAD_DIM:(h + 1) * HEAD_DIM, :]
        qp_ref[h] = jnp.concatenate([qh, zeros] if h < GQA_GROUP else [zeros, qh], axis=0)
    m_ref[...] = jnp.full(m_ref.shape, -jnp.inf, jnp.float32)
    acc_ref[...] = jnp.zeros(acc_ref.shape, jnp.float32)

    def scores(c, h):
        c0 = pl.multiple_of(c * tk, tk)
        s_ref[h % 2] = jnp.dot(k_ref[0, pl.ds(c0, tk), :], qp_ref[h],
                               preferred_element_type=jnp.float32)

    scores(0, 0)

    def chunk(c, carry):
        c0 = pl.multiple_of(c * tk, tk)
        for h in range(N_HEADS):
            if h + 1 < N_HEADS:
                scores(c, h + 1)
            else:
                scores(jnp.minimum(c + 1, n_chunks - 1), 0)
            s = s_ref[h % 2]
            m_old = m_ref[h]
            m_new = jnp.maximum(m_old, jnp.max(s, axis=0, keepdims=True))
            alpha = jnp.exp2(m_old - m_new)
            p = jnp.exp2(s - m_new).astype(jnp.bfloat16)
            vc = v_ref[0, h // GQA_GROUP, :, pl.ds(c0, tk)]
            acc_ref[h] = alpha * acc_ref[h] + jnp.dot(vc, p, preferred_element_type=jnp.float32)
            m_ref[h] = m_new
        return carry

    lax.fori_loop(0, n_chunks, chunk, 0)
    for h in range(N_HEADS):
        a = acc_ref[h]
        o_ref[0, h * HEAD_DIM:(h + 1) * HEAD_DIM, :] = (
            a[0:HEAD_DIM] / a[HEAD_DIM:HEAD_DIM + 1]).astype(jnp.bfloat16)


def _attention(qt, k, vt, tq, tk):
    B, _, S = qt.shape
    return pl.pallas_call(
        functools.partial(_attn_kernel, tk=tk),
        grid=(B, S // tq),
        in_specs=[
            pl.BlockSpec((1, ATTN_WIDTH, tq), lambda b, i: (b, 0, i)),
            pl.BlockSpec((1, S, KV_WIDTH), lambda b, i: (b, 0, 0)),
            pl.BlockSpec((1, N_KV_HEADS, V_ROWS, S), lambda b, i: (b, 0, 0, 0)),
        ],
        out_specs=pl.BlockSpec((1, ATTN_WIDTH, tq), lambda b, i: (b, 0, i)),
        out_shape=jax.ShapeDtypeStruct((B, ATTN_WIDTH, S), jnp.bfloat16),
        scratch_shapes=[
            pltpu.VMEM((N_HEADS, KV_WIDTH, tq), jnp.bfloat16),
            pltpu.VMEM((N_HEADS, 1, tq), jnp.float32),
            pltpu.VMEM((N_HEADS, V_ROWS, tq), jnp.float32),
            pltpu.VMEM((2, tk, tq), jnp.float32),
        ],
        compiler_params=_cp(("parallel", "parallel")),
        name="attention",
    )(qt, k, vt)


def _mix_kernel(x_ref, at_ref, uc_ref, up_ref, un_ref, wp_ref, ps_ref, woa_ref, wop_ref,
                lg_ref, lb_ref, g1_ref, b1_ref, wr_ref, h_ref, aff_ref, ue_ref, *, seq_len):
    i = pl.program_id(1)
    T = uc_ref.shape[1]
    n_tiles = seq_len // T
    ue_ref[0:HALO, :] = jnp.where(i > 0, up_ref[0], 0.0)
    ue_ref[HALO:HALO + T, :] = uc_ref[0]
    ue_ref[HALO + T:, :] = jnp.where(i < n_tiles - 1, un_ref[0], 0.0)
    tg = (i * T + lax.broadcasted_iota(jnp.int32, (T, 1), 0))
    pooled = []
    for g, w in enumerate(POOL_WINDOWS):
        cols = slice(g * POOL_GROUP_W, (g + 1) * POOL_GROUP_W)
        win = ue_ref[HALO - w // 2:HALO - w // 2 + T, cols]
        for d in range(1 - w // 2, w - w // 2):
            win = win + ue_ref[HALO + d:HALO + d + T, cols]
        lo = jnp.maximum(tg - w // 2, 0)
        hi = jnp.minimum(tg - w // 2 + w, seq_len)
        cnt = (hi - lo).astype(jnp.float32)
        m = win / cnt - uc_ref[0, :, cols]
        y = jnp.dot(m.astype(jnp.bfloat16), wp_ref[g], preferred_element_type=jnp.float32)
        pooled.append(y * ps_ref[:, cols])
    pool = jnp.concatenate(pooled, axis=1).astype(jnp.bfloat16)
    mix = lax.dot_general(at_ref[0], woa_ref[...], (((0,), (0,)), ((), ())),
                          preferred_element_type=jnp.float32)
    mix = mix + jnp.dot(pool, wop_ref[...], preferred_element_type=jnp.float32)
    x0 = _ln(x_ref[0], lg_ref[...], lb_ref[...])
    h = _ln(ALPHA * x0 + mix, g1_ref[...], b1_ref[...])
    h_ref[0] = h
    logits = lax.dot_general(wr_ref[...], h.astype(jnp.bfloat16), (((1,), (1,)), ((), ())),
                             preferred_element_type=jnp.float32)
    e = jnp.exp(logits - jnp.max(logits, axis=0, keepdims=True))
    aff_ref[...] = e / jnp.sum(e, axis=0, keepdims=True)


def _mix_out(x, attn_t, u, wp, ps, woa, wop, lg, lb, g1, b1, wr_t, tile):
    B, S, _ = x.shape
    nt = S // tile
    hb = tile // HALO
    const = lambda shape: pl.BlockSpec(shape, lambda b, i: tuple(0 for _ in shape))
    return pl.pallas_call(
        functools.partial(_mix_kernel, seq_len=S),
        grid=(B, nt),
        in_specs=[
            pl.BlockSpec((1, tile, D_MODEL), lambda b, i: (b, i, 0)),
            pl.BlockSpec((1, ATTN_WIDTH, tile), lambda b, i: (b, 0, i)),
            pl.BlockSpec((1, tile, POOL_WIDTH), lambda b, i: (b, i, 0)),
            pl.BlockSpec((1, HALO, POOL_WIDTH), lambda b, i: (b, jnp.maximum(i * hb - 1, 0), 0)),
            pl.BlockSpec((1, HALO, POOL_WIDTH),
                         lambda b, i: (b, jnp.minimum((i + 1) * hb, S // HALO - 1), 0)),
            const(wp.shape), const((1, POOL_WIDTH)),
            const((ATTN_WIDTH, D_MODEL)), const((POOL_WIDTH, D_MODEL)),
            const((1, D_MODEL)), const((1, D_MODEL)), const((1, D_MODEL)), const((1, D_MODEL)),
            const((N_EXPERTS, D_MODEL)),
        ],
        out_specs=[
            pl.BlockSpec((1, tile, D_MODEL), lambda b, i: (b, i, 0)),
            pl.BlockSpec((N_EXPERTS, tile), lambda b, i: (0, b * nt + i)),
        ],
        out_shape=[
            jax.ShapeDtypeStruct((B, S, D_MODEL), jnp.float32),
            jax.ShapeDtypeStruct((N_EXPERTS, B * S), jnp.float32),
        ],
        scratch_shapes=[pltpu.VMEM((tile + 2 * HALO, POOL_WIDTH), jnp.float32)],
        compiler_params=_cp(("parallel", "parallel")),
        name="mix_out",
    )(x, attn_t, u, u, u, wp, ps, woa, wop, lg, lb, g1, b1, wr_t)


def _route_kernel(aff_ref, tri_ref, ones_ref, low_ref, pos_ref, roff_ref, thr_ref, *, cap):
    n_e, R, _ = aff_ref.shape
    tri = tri_ref[...]
    ones = ones_ref[...]
    low = low_ref[...]

    def excl_prefix(flag):
        fb = flag.astype(jnp.bfloat16)
        incl = jnp.dot(fb, tri, preferred_element_type=jnp.float32)
        tot = jnp.dot(fb, ones, preferred_element_type=jnp.float32)
        before = jnp.dot(low, tot.astype(jnp.bfloat16), preferred_element_type=jnp.float32)
        return before + incl - flag, before

    for e in range(n_e):
        thr_ref[e] = 0

    def bit_step(it, carry):
        bit = 30 - it
        for e in range(n_e):
            keys = pltpu.bitcast(aff_ref[e], jnp.int32)
            cand = thr_ref[e] | lax.shift_left(jnp.int32(1), bit)
            cnt = jnp.sum(jnp.where(keys >= cand, 1.0, 0.0))
            thr_ref[e] = jnp.where(cnt >= float(cap), cand, thr_ref[e])
        return carry

    lax.fori_loop(0, 31, bit_step, 0)

    for e in range(n_e):
        keys = pltpu.bitcast(aff_ref[e], jnp.int32)
        thr = thr_ref[e]
        gt = jnp.where(keys > thr, 1.0, 0.0)
        eq = jnp.where(keys == thr, 1.0, 0.0)
        need = float(cap) - jnp.sum(gt)
        eq_rank, _ = excl_prefix(eq)
        sel = gt + eq * jnp.where(eq_rank < need, 1.0, 0.0)
        pos, before = excl_prefix(sel)
        pos_ref[e] = jnp.where(sel > 0.5, pos, -1.0).astype(jnp.int32)
        roff_ref[e] = before.astype(jnp.int32)


def _route(aff3, cap):
    n_e, R, _ = aff3.shape
    tri = jnp.asarray(np.triu(np.ones((LANES, LANES), np.float32)), jnp.bfloat16)
    ones = jnp.ones((LANES, LANES), jnp.bfloat16)
    low = jnp.asarray(np.tril(np.ones((R, R), np.float32), -1), jnp.bfloat16)
    return pl.pallas_call(
        functools.partial(_route_kernel, cap=cap),
        out_shape=[jax.ShapeDtypeStruct((n_e, R, LANES), jnp.int32),
                   jax.ShapeDtypeStruct((n_e, R, LANES), jnp.int32)],
        scratch_shapes=[pltpu.SMEM((n_e,), jnp.int32)],
        compiler_params=pltpu.CompilerParams(vmem_limit_bytes=V7X_VMEM_LIMIT),
        name="route",
    )(aff3, tri, ones, low)


def _slot_base(off_ref, e, j, p, nt1):
    a = off_ref[e * nt1 + j]
    a16 = lax.shift_left(lax.shift_right_logical(a, 4), 4)
    return a16 + p * SLOT_WIN


def _window_onehot(pos_ref, rel_bases, min_pos, tile, value_rows=None):
    rows = lax.broadcasted_iota(jnp.int32, (SLOT_WIN, tile), 0)
    blocks = []
    for e in range(N_EXPERTS):
        pos = pos_ref[e:e + 1, :]
        rel = jnp.where(pos >= min_pos[e], pos - rel_bases[e], -1)
        hit = rows == rel
        if value_rows is None:
            blocks.append(jnp.where(hit, 1.0, 0.0).astype(jnp.bfloat16))
        else:
            blocks.append(jnp.where(hit, value_rows[e:e + 1, :], 0.0).astype(jnp.bfloat16))
    return jnp.concatenate(blocks, axis=0)


def _gather_kernel(off_ref, np_ref, pos_ref, h_ref, xe_ref, stage_ref, carry_ref, pend_ref, sem,
                   *, nt, cap):
    j = pl.program_id(0)
    nt1 = nt + 1
    tile = h_ref.shape[0]

    @pl.when(j == 0)
    def _():
        carry_ref[...] = jnp.zeros(carry_ref.shape, jnp.bfloat16)
        for e in range(N_EXPERTS):
            pend_ref[e] = 0

    hb = h_ref[...].astype(jnp.bfloat16)

    def flush_copy(e, base):
        return pltpu.make_async_copy(stage_ref.at[e], xe_ref.at[e, pl.ds(base, SLOT_WIN)], sem.at[e])

    def one_pass(p, carry):
        bases = [_slot_base(off_ref, e, j, p, nt1) for e in range(N_EXPERTS)]
        onehot = _window_onehot(pos_ref, bases, bases, tile)
        rows = jnp.dot(onehot, hb, preferred_element_type=jnp.float32).astype(jnp.bfloat16)
        for e in range(N_EXPERTS):
            b = off_ref[e * nt1 + j + 1]

            @pl.when(bases[e] <= b)
            def _(e=e, b=b):
                @pl.when(pend_ref[e] == 1)
                def _():
                    flush_copy(e, 0).wait()
                stage_ref[e] = rows[e * SLOT_WIN:(e + 1) * SLOT_WIN]

                @pl.when(p == 0)
                def _():
                    stage_ref[e, 0:BF16_SUBLANES, :] = (stage_ref[e, 0:BF16_SUBLANES, :]
                                                        + carry_ref[e])
                flush_copy(e, pl.multiple_of(bases[e], BF16_SUBLANES)).start()
                pend_ref[e] = 1

                @pl.when(b < bases[e] + SLOT_WIN)
                def _():
                    g = lax.shift_left(lax.shift_right_logical(b, 4), 4) - bases[e]
                    carry_ref[e] = stage_ref[e, pl.ds(pl.multiple_of(g, BF16_SUBLANES),
                                                      BF16_SUBLANES), :]
        return carry

    lax.fori_loop(0, np_ref[j], one_pass, 0)

    @pl.when(j == nt - 1)
    def _():
        for e in range(N_EXPERTS):
            @pl.when(pend_ref[e] == 1)
            def _(e=e):
                flush_copy(e, 0).wait()
                pend_ref[e] = 0
        for e in range(N_EXPERTS):
            stage_ref[e] = jnp.zeros((SLOT_WIN, D_MODEL), jnp.bfloat16)
            flush_copy(e, cap).start()
        for e in range(N_EXPERTS):
            flush_copy(e, cap).wait()


def _gather(off, npass, pos, h, cap):
    N = h.shape[0]
    nt = N // TOK_TILE
    return pl.pallas_call(
        functools.partial(_gather_kernel, nt=nt, cap=cap),
        grid_spec=pltpu.PrefetchScalarGridSpec(
            num_scalar_prefetch=2,
            grid=(nt,),
            in_specs=[
                pl.BlockSpec((N_EXPERTS, TOK_TILE), lambda j, *_: (0, j)),
                pl.BlockSpec((TOK_TILE, D_MODEL), lambda j, *_: (j, 0)),
            ],
            out_specs=pl.BlockSpec(memory_space=pl.ANY),
            scratch_shapes=[
                pltpu.VMEM((N_EXPERTS, SLOT_WIN, D_MODEL), jnp.bfloat16),
                pltpu.VMEM((N_EXPERTS, BF16_SUBLANES, D_MODEL), jnp.bfloat16),
                pltpu.SMEM((N_EXPERTS,), jnp.int32),
                pltpu.SemaphoreType.DMA((N_EXPERTS,)),
            ],
        ),
        out_shape=jax.ShapeDtypeStruct((N_EXPERTS, cap + SLOT_WIN, D_MODEL), jnp.bfloat16),
        compiler_params=_cp(("arbitrary",)),
        name="gather",
    )(off, npass, pos, h)


def _ffn_kernel(x_ref, wg_ref, wu_ref, wd_ref, y_ref):
    x = x_ref[0]
    half = D_FF // 2
    acc = None
    for c in range(2):
        cols = slice(c * half, (c + 1) * half)
        g = jnp.dot(x, wg_ref[0, :, cols], preferred_element_type=jnp.float32)
        u = jnp.dot(x, wu_ref[0, :, cols], preferred_element_type=jnp.float32)
        hid = (g * jax.nn.sigmoid(g) * u).astype(jnp.bfloat16)
        part = jnp.dot(hid, wd_ref[0, cols, :], preferred_element_type=jnp.float32)
        acc = part if acc is None else acc + part
    y_ref[0] = acc.astype(jnp.bfloat16)


def _ffn(xe, wg, wu, wd, cap, tile):
    n_e = xe.shape[0]
    return pl.pallas_call(
        _ffn_kernel,
        grid=(n_e, cap // tile),
        in_specs=[
            pl.BlockSpec((1, tile, D_MODEL), lambda e, s: (e, s, 0)),
            pl.BlockSpec((1, D_MODEL, D_FF), lambda e, s: (e, 0, 0)),
            pl.BlockSpec((1, D_MODEL, D_FF), lambda e, s: (e, 0, 0)),
            pl.BlockSpec((1, D_FF, D_MODEL), lambda e, s: (e, 0, 0)),
        ],
        out_specs=pl.BlockSpec((1, tile, D_MODEL), lambda e, s: (e, s, 0)),
        out_shape=jax.ShapeDtypeStruct((n_e, cap, D_MODEL), jnp.bfloat16),
        compiler_params=_cp(("parallel", "arbitrary")),
        name="ffn",
    )(xe, wg, wu, wd)


def _combine_kernel(off_ref, np_ref, pos_ref, aff_ref, h_ref, g2_ref, b2_ref, ye_ref, y_ref,
                    win_ref, slot_ref, sem, *, nt, cap):
    j = pl.program_id(0)
    nt1 = nt + 1
    tile = h_ref.shape[0]

    def fetch_copy(e, base, slot):
        return pltpu.make_async_copy(ye_ref.at[e, pl.ds(base, SLOT_WIN)], win_ref.at[slot, e],
                                     sem.at[slot])

    def read_base(e, jj, p):
        return jnp.minimum(_slot_base(off_ref, e, jj, p, nt1), cap - SLOT_WIN)

    def start_fetch(jj, p, slot):
        for e in range(N_EXPERTS):
            fetch_copy(e, pl.multiple_of(read_base(e, jj, p), BF16_SUBLANES), slot).start()

    def wait_fetch(slot):
        for e in range(N_EXPERTS):
            fetch_copy(e, 0, slot).wait()

    @pl.when(j == 0)
    def _():
        slot_ref[0] = 0
        start_fetch(0, 0, 0)

    n_pass = np_ref[j]
    gates = aff_ref[...]

    def one_pass(p, ff):
        slot = slot_ref[0]
        wait_fetch(slot)

        @pl.when(p + 1 < n_pass)
        def _():
            start_fetch(j, p + 1, 1 - slot)

        @pl.when(jnp.logical_and(p + 1 >= n_pass, j + 1 < nt))
        def _():
            start_fetch(j + 1, 0, 1 - slot)

        onehot = _window_onehot(pos_ref,
                                [read_base(e, j, p) for e in range(N_EXPERTS)],
                                [_slot_base(off_ref, e, j, p, nt1) for e in range(N_EXPERTS)],
                                tile, value_rows=gates)
        rows = win_ref[slot].reshape(N_EXPERTS * SLOT_WIN, D_MODEL)
        ff = ff + lax.dot_general(onehot, rows, (((0,), (0,)), ((), ())),
                                  preferred_element_type=jnp.float32)
        slot_ref[0] = 1 - slot
        return ff

    ff = lax.fori_loop(0, n_pass, one_pass, jnp.zeros((tile, D_MODEL), jnp.float32))
    y_ref[...] = _ln(ALPHA * h_ref[...] + ff, g2_ref[...], b2_ref[...])


def _combine(off, npass, pos, aff, h, g2, b2, ye, cap):
    N = h.shape[0]
    nt = N // TOK_TILE
    return pl.pallas_call(
        functools.partial(_combine_kernel, nt=nt, cap=cap),
        grid_spec=pltpu.PrefetchScalarGridSpec(
            num_scalar_prefetch=2,
            grid=(nt,),
            in_specs=[
                pl.BlockSpec((N_EXPERTS, TOK_TILE), lambda j, *_: (0, j)),
                pl.BlockSpec((N_EXPERTS, TOK_TILE), lambda j, *_: (0, j)),
                pl.BlockSpec((TOK_TILE, D_MODEL), lambda j, *_: (j, 0)),
                pl.BlockSpec((1, D_MODEL), lambda j, *_: (0, 0)),
                pl.BlockSpec((1, D_MODEL), lambda j, *_: (0, 0)),
                pl.BlockSpec(memory_space=pl.ANY),
            ],
            out_specs=pl.BlockSpec((TOK_TILE, D_MODEL), lambda j, *_: (j, 0)),
            scratch_shapes=[
                pltpu.VMEM((2, N_EXPERTS, SLOT_WIN, D_MODEL), jnp.bfloat16),
                pltpu.SMEM((1,), jnp.int32),
                pltpu.SemaphoreType.DMA((2,)),
            ],
        ),
        out_shape=jax.ShapeDtypeStruct((N, D_MODEL), jnp.float32),
        compiler_params=_cp(("arbitrary",)),
        name="combine",
    )(off, npass, pos, aff, h, g2, b2, ye)


def _rope_tables(seq_len):
    rows = seq_len // GRID_W
    row = jnp.repeat(jnp.arange(rows), GRID_W).astype(jnp.float32)
    col = jnp.tile(jnp.arange(GRID_W), rows).astype(jnp.float32)
    inv_freq = ROPE_THETA ** (-jnp.arange(ROT_PAIRS, dtype=jnp.float32) / ROT_PAIRS)
    ang_r = inv_freq[:, None] * row[None, :]
    ang_c = inv_freq[:, None] * col[None, :]
    ct = jnp.concatenate([jnp.cos(ang_r), jnp.cos(ang_r), jnp.cos(ang_c), jnp.cos(ang_c)], axis=0)
    st = jnp.concatenate([-jnp.sin(ang_r), jnp.sin(ang_r), -jnp.sin(ang_c), jnp.sin(ang_c)], axis=0)
    return ct, st


def _trunk(x, prm, *, proj_tile, attn_tq, attn_tk, mix_tile, ffn_tile):
    B, S, _ = x.shape
    N = B * S
    cap = (CAPACITY_FACTOR * N) // N_EXPERTS
    assert S % proj_tile == 0 and S % attn_tq == 0 and S % attn_tk == 0 and S % mix_tile == 0
    assert N % TOK_TILE == 0 and TOK_TILE % LANES == 0
    assert cap % BF16_SUBLANES == 0 and cap >= SLOT_WIN and cap % ffn_tile == 0

    ct, st = _rope_tables(S)
    qt, k, vt, u = _in_proj(x, prm["ln_in_g"], prm["ln_in_b"], prm["wt"], prm["wu"], ct, st,
                            prm["gq"], prm["gk"], proj_tile)
    attn_t = _attention(qt, k, vt, attn_tq, attn_tk)
    h, aff = _mix_out(x, attn_t, u, prm["w_pool"], prm["pool_scale"],
                      prm["wo_attn"], prm["wo_pool"], prm["ln_in_g"], prm["ln_in_b"],
                      prm["ln1_g"], prm["ln1_b"], prm["wr_t"], mix_tile)
    h = h.reshape(N, D_MODEL)

    pos3, roff3 = _route(aff.reshape(N_EXPERTS, N // LANES, LANES), cap)
    pos = pos3.reshape(N_EXPERTS, N)
    off = jnp.concatenate([roff3[:, ::TOK_TILE // LANES, 0],
                           jnp.full((N_EXPERTS, 1), cap, jnp.int32)], axis=1)
    a, b = off[:, :-1], off[:, 1:]
    a16 = (a // BF16_SUBLANES) * BF16_SUBLANES
    npass = jnp.max((b - a16) // SLOT_WIN + 1, axis=0).astype(jnp.int32)
    off = off.reshape(-1)

    xe = _gather(off, npass, pos, h, cap)
    ye = _ffn(xe, prm["w_gate"], prm["w_up"], prm["w_down"], cap, ffn_tile)
    y = _combine(off, npass, pos, aff, h, prm["ln2_g"], prm["ln2_b"], ye, cap)
    return y.reshape(B, S, D_MODEL)


def _prepare(ln_in_g, ln_in_b, w_in, q_norm_g, k_norm_g, w_pool, pool_scale, w_out, ln1_g, ln1_b,
             w_router, w_gate, w_up, w_down, ln2_g, ln2_b):
    bf = jnp.bfloat16
    row = lambda v: v.reshape(1, -1).astype(jnp.float32)
    qkv = ATTN_WIDTH + 2 * KV_WIDTH
    return {
        "ln_in_g": row(ln_in_g), "ln_in_b": row(ln_in_b),
        "wt": jnp.transpose(w_in[0][:, :qkv]).astype(bf),
        "wu": w_in[0][:, qkv:].astype(bf),
        "gq": q_norm_g[0].reshape(HEAD_DIM, 1), "gk": k_norm_g[0].reshape(HEAD_DIM, 1),
        "w_pool": w_pool[0].astype(bf), "pool_scale": row(pool_scale[0]),
        "wo_attn": w_out[0][:ATTN_WIDTH].astype(bf), "wo_pool": w_out[0][ATTN_WIDTH:].astype(bf),
        "ln1_g": row(ln1_g[0]), "ln1_b": row(ln1_b[0]),
        "wr_t": jnp.transpose(w_router[0]).astype(bf),
        "w_gate": w_gate[0].astype(bf), "w_up": w_up[0].astype(bf), "w_down": w_down[0].astype(bf),
        "ln2_g": row(ln2_g[0]), "ln2_b": row(ln2_b[0]),
    }


def kernel(x_prompt, x_sample, ln_in_g, ln_in_b, w_in, q_norm_g, k_norm_g, w_pool, pool_scale, w_out,
           ln1_g, ln1_b, w_router, w_gate, w_up, w_down, ln2_g, ln2_b):
    prm = _prepare(ln_in_g, ln_in_b, w_in, q_norm_g, k_norm_g, w_pool, pool_scale, w_out, ln1_g, ln1_b,
                   w_router, w_gate, w_up, w_down, ln2_g, ln2_b)
    tiles = dict(proj_tile=512, attn_tq=512, attn_tk=512, mix_tile=256, ffn_tile=512)
    return (_trunk(x_prompt, prm, **tiles), _trunk(x_sample, prm, **tiles))
```

```python
import functools
import math

import jax
import jax.numpy as jnp
import numpy as np
from jax import lax
from jax.experimental import pallas as pl
from jax.experimental.pallas import tpu as pltpu

D_MODEL = 1024
ATTN_WIDTH = 512
POOL_WIDTH = 512
HEAD_DIM = 64
N_HEADS = 8
N_KV_HEADS = 2
GQA_GROUP = N_HEADS // N_KV_HEADS
KV_WIDTH = N_KV_HEADS * HEAD_DIM
ROT_PAIRS = HEAD_DIM // 4
ROPE_THETA = 10000.0
GRID_W = 64
POOL_WINDOWS = (2, 4, 8, 16)
POOL_GROUP_W = 128
N_EXPERTS = 16
CAPACITY_FACTOR = 2
D_FF = 2048
LN_EPS = 1e-5
RMS_EPS = 1e-6
ALPHA = 2.0 ** 0.25

LANES = 128
BF16_SUBLANES = 16
V7X_VMEM_LIMIT = 56 * 1024 * 1024

HALO = 16
V_ROWS = 80
SLOT_WIN = 64
TOK_TILE = 256


def _ln(x, g, b):
    mu = jnp.mean(x, axis=-1, keepdims=True)
    xc = x - mu
    var = jnp.mean(xc * xc, axis=-1, keepdims=True)
    return xc * lax.rsqrt(var + LN_EPS) * g + b


def _cp(sem, vmem=V7X_VMEM_LIMIT):
    return pltpu.CompilerParams(dimension_semantics=sem, vmem_limit_bytes=vmem)


def _in_proj_kernel(x_ref, g_ref, b_ref, wt_ref, wu_ref, ct_ref, st_ref, gq_ref, gk_ref,
                    q_ref, k_ref, v_ref, u_ref, *, q_scale):
    x0 = _ln(x_ref[0], g_ref[...], b_ref[...])
    xb = x0.astype(jnp.bfloat16)
    pt = lax.dot_general(wt_ref[...], xb, (((1,), (1,)), ((), ())),
                         preferred_element_type=jnp.float32)
    u_ref[0] = jnp.dot(xb, wu_ref[...], preferred_element_type=jnp.float32)
    ct = ct_ref[...]
    st = st_ref[...]

    def norm_rope(xh, gcol):
        ms = jnp.mean(xh * xh, axis=0, keepdims=True)
        xn = xh * lax.rsqrt(ms + RMS_EPS) * gcol
        partner = jnp.concatenate([xn[16:32], xn[0:16], xn[48:64], xn[32:48]], axis=0)
        return xn * ct + partner * st

    gq = gq_ref[...]
    gk = gk_ref[...]
    for h in range(N_HEADS):
        qh = norm_rope(pt[h * HEAD_DIM:(h + 1) * HEAD_DIM], gq) * q_scale
        q_ref[0, h * HEAD_DIM:(h + 1) * HEAD_DIM, :] = qh.astype(jnp.bfloat16)
    kt = jnp.concatenate(
        [norm_rope(pt[ATTN_WIDTH + h * HEAD_DIM:ATTN_WIDTH + (h + 1) * HEAD_DIM], gk)
         for h in range(N_KV_HEADS)], axis=0)
    k_ref[0] = jnp.transpose(kt).astype(jnp.bfloat16)
    ones = jnp.ones((V_ROWS - HEAD_DIM, pt.shape[1]), jnp.bfloat16)
    for h in range(N_KV_HEADS):
        r0 = ATTN_WIDTH + KV_WIDTH + h * HEAD_DIM
        v_ref[0, h, 0:HEAD_DIM, :] = pt[r0:r0 + HEAD_DIM].astype(jnp.bfloat16)
        v_ref[0, h, HEAD_DIM:V_ROWS, :] = ones


def _in_proj(x, ln_g, ln_b, wt, wu, ct, st, gq, gk, tile):
    B, S, _ = x.shape
    q_scale = (HEAD_DIM ** -0.5) * math.log2(math.e)
    const = lambda shape: pl.BlockSpec(shape, lambda b, i: tuple(0 for _ in shape))
    return pl.pallas_call(
        functools.partial(_in_proj_kernel, q_scale=q_scale),
        grid=(B, S // tile),
        in_specs=[
            pl.BlockSpec((1, tile, D_MODEL), lambda b, i: (b, i, 0)),
            const((1, D_MODEL)), const((1, D_MODEL)),
            const((ATTN_WIDTH + 2 * KV_WIDTH, D_MODEL)),
            const((D_MODEL, POOL_WIDTH)),
            pl.BlockSpec((HEAD_DIM, tile), lambda b, i: (0, i)),
            pl.BlockSpec((HEAD_DIM, tile), lambda b, i: (0, i)),
            const((HEAD_DIM, 1)), const((HEAD_DIM, 1)),
        ],
        out_specs=[
            pl.BlockSpec((1, ATTN_WIDTH, tile), lambda b, i: (b, 0, i)),
            pl.BlockSpec((1, tile, KV_WIDTH), lambda b, i: (b, i, 0)),
            pl.BlockSpec((1, N_KV_HEADS, V_ROWS, tile), lambda b, i: (b, 0, 0, i)),
            pl.BlockSpec((1, tile, POOL_WIDTH), lambda b, i: (b, i, 0)),
        ],
        out_shape=[
            jax.ShapeDtypeStruct((B, ATTN_WIDTH, S), jnp.bfloat16),
            jax.ShapeDtypeStruct((B, S, KV_WIDTH), jnp.bfloat16),
            jax.ShapeDtypeStruct((B, N_KV_HEADS, V_ROWS, S), jnp.bfloat16),
            jax.ShapeDtypeStruct((B, S, POOL_WIDTH), jnp.float32),
        ],
        compiler_params=_cp(("parallel", "parallel")),
        name="in_proj",
    )(x, ln_g, ln_b, wt, wu, ct, st, gq, gk)


def _attn_kernel(q_ref, k_ref, v_ref, o_ref, qp_ref, m_ref, acc_ref, s_ref, cmax_ref, *, tk):
    S = k_ref.shape[1]
    tq = q_ref.shape[2]
    n_chunks = S // tk
    zeros = jnp.zeros((HEAD_DIM, tq), jnp.bfloat16)
    for h in range(N_HEADS):
        qh = q_ref[0, h * HEAD_DIM:(h + 1) * HEAD_DIM, :]
        qp_ref[h] = jnp.concatenate([qh, zeros] if h < GQA_GROUP else [zeros, qh], axis=0)
    m_ref[...] = jnp.full(m_ref.shape, -jnp.inf, jnp.float32)
    acc_ref[...] = jnp.zeros(acc_ref.shape, jnp.float32)

    def scores(c, h):
        c0 = pl.multiple_of(c * tk, tk)
        s = jnp.dot(k_ref[0, pl.ds(c0, tk), :], qp_ref[h],
                    preferred_element_type=jnp.float32)
        s_ref[h % 2] = s
        cmax_ref[h % 2] = jnp.max(s, axis=0, keepdims=True)

    scores(0, 0)

    def chunk(c, carry):
        c0 = pl.multiple_of(c * tk, tk)
        for h in range(N_HEADS):
            if h + 1 < N_HEADS:
                scores(c, h + 1)
            else:
                scores(jnp.minimum(c + 1, n_chunks - 1), 0)
            m_old = m_ref[h]
            m_new = jnp.maximum(m_old, cmax_ref[h % 2])
            alpha = jnp.exp2(m_old - m_new)
            p = jnp.exp2(s_ref[h % 2] - m_new).astype(jnp.bfloat16)
            vc = v_ref[0, h // GQA_GROUP, :, pl.ds(c0, tk)]
            acc_ref[h] = alpha * acc_ref[h] + jnp.dot(vc, p, preferred_element_type=jnp.float32)
            m_ref[h] = m_new
        return carry

    lax.fori_loop(0, n_chunks, chunk, 0)
    for h in range(N_HEADS):
        a = acc_ref[h]
        o_ref[0, h * HEAD_DIM:(h + 1) * HEAD_DIM, :] = (
            a[0:HEAD_DIM] / a[HEAD_DIM:HEAD_DIM + 1]).astype(jnp.bfloat16)


def _attention(qt, k, vt, tq, tk):
    B, _, S = qt.shape
    return pl.pallas_call(
        functools.partial(_attn_kernel, tk=tk),
        grid=(B, S // tq),
        in_specs=[
            pl.BlockSpec((1, ATTN_WIDTH, tq), lambda b, i: (b, 0, i)),
            pl.BlockSpec((1, S, KV_WIDTH), lambda b, i: (b, 0, 0)),
            pl.BlockSpec((1, N_KV_HEADS, V_ROWS, S), lambda b, i: (b, 0, 0, 0)),
        ],
        out_specs=pl.BlockSpec((1, ATTN_WIDTH, tq), lambda b, i: (b, 0, i)),
        out_shape=jax.ShapeDtypeStruct((B, ATTN_WIDTH, S), jnp.bfloat16),
        scratch_shapes=[
            pltpu.VMEM((N_HEADS, KV_WIDTH, tq), jnp.bfloat16),
            pltpu.VMEM((N_HEADS, 1, tq), jnp.float32),
            pltpu.VMEM((N_HEADS, V_ROWS, tq), jnp.float32),
            pltpu.VMEM((2, tk, tq), jnp.float32),
            pltpu.VMEM((2, 1, tq), jnp.float32),
        ],
        compiler_params=_cp(("parallel", "parallel")),
        name="attention",
    )(qt, k, vt)


def _mix_kernel(x_ref, at_ref, uc_ref, up_ref, un_ref, wp_ref, ps_ref, woa_ref, wop_ref,
                lg_ref, lb_ref, g1_ref, b1_ref, wr_ref, h_ref, aff_ref, ue_ref, *, seq_len):
    i = pl.program_id(1)
    T = uc_ref.shape[1]
    n_tiles = seq_len // T
    ue_ref[0:HALO, :] = jnp.where(i > 0, up_ref[0], 0.0)
    ue_ref[HALO:HALO + T, :] = uc_ref[0]
    ue_ref[HALO + T:, :] = jnp.where(i < n_tiles - 1, un_ref[0], 0.0)
    tg = (i * T + lax.broadcasted_iota(jnp.int32, (T, 1), 0))
    pooled = []
    for g, w in enumerate(POOL_WINDOWS):
        cols = slice(g * POOL_GROUP_W, (g + 1) * POOL_GROUP_W)
        win = ue_ref[HALO - w // 2:HALO - w // 2 + T, cols]
        for d in range(1 - w // 2, w - w // 2):
            win = win + ue_ref[HALO + d:HALO + d + T, cols]
        lo = jnp.maximum(tg - w // 2, 0)
        hi = jnp.minimum(tg - w // 2 + w, seq_len)
        cnt = (hi - lo).astype(jnp.float32)
        m = win / cnt - uc_ref[0, :, cols]
        y = jnp.dot(m.astype(jnp.bfloat16), wp_ref[g], preferred_element_type=jnp.float32)
        pooled.append(y * ps_ref[:, cols])
    pool = jnp.concatenate(pooled, axis=1).astype(jnp.bfloat16)
    mix = lax.dot_general(at_ref[0], woa_ref[...], (((0,), (0,)), ((), ())),
                          preferred_element_type=jnp.float32)
    mix = mix + jnp.dot(pool, wop_ref[...], preferred_element_type=jnp.float32)
    x0 = _ln(x_ref[0], lg_ref[...], lb_ref[...])
    h = _ln(ALPHA * x0 + mix, g1_ref[...], b1_ref[...])
    h_ref[0] = h
    logits = lax.dot_general(wr_ref[...], h.astype(jnp.bfloat16), (((1,), (1,)), ((), ())),
                             preferred_element_type=jnp.float32)
    e = jnp.exp(logits - jnp.max(logits, axis=0, keepdims=True))
    aff_ref[...] = e / jnp.sum(e, axis=0, keepdims=True)


def _mix_out(x, attn_t, u, wp, ps, woa, wop, lg, lb, g1, b1, wr_t, tile):
    B, S, _ = x.shape
    nt = S // tile
    hb = tile // HALO
    const = lambda shape: pl.BlockSpec(shape, lambda b, i: tuple(0 for _ in shape))
    return pl.pallas_call(
        functools.partial(_mix_kernel, seq_len=S),
        grid=(B, nt),
        in_specs=[
            pl.BlockSpec((1, tile, D_MODEL), lambda b, i: (b, i, 0)),
            pl.BlockSpec((1, ATTN_WIDTH, tile), lambda b, i: (b, 0, i)),
            pl.BlockSpec((1, tile, POOL_WIDTH), lambda b, i: (b, i, 0)),
            pl.BlockSpec((1, HALO, POOL_WIDTH), lambda b, i: (b, jnp.maximum(i * hb - 1, 0), 0)),
            pl.BlockSpec((1, HALO, POOL_WIDTH),
                         lambda b, i: (b, jnp.minimum((i + 1) * hb, S // HALO - 1), 0)),
            const(wp.shape), const((1, POOL_WIDTH)),
            const((ATTN_WIDTH, D_MODEL)), const((POOL_WIDTH, D_MODEL)),
            const((1, D_MODEL)), const((1, D_MODEL)), const((1, D_MODEL)), const((1, D_MODEL)),
            const((N_EXPERTS, D_MODEL)),
        ],
        out_specs=[
            pl.BlockSpec((1, tile, D_MODEL), lambda b, i: (b, i, 0)),
            pl.BlockSpec((N_EXPERTS, tile), lambda b, i: (0, b * nt + i)),
        ],
        out_shape=[
            jax.ShapeDtypeStruct((B, S, D_MODEL), jnp.float32),
            jax.ShapeDtypeStruct((N_EXPERTS, B * S), jnp.float32),
        ],
        scratch_shapes=[pltpu.VMEM((tile + 2 * HALO, POOL_WIDTH), jnp.float32)],
        compiler_params=_cp(("parallel", "parallel")),
        name="mix_out",
    )(x, attn_t, u, u, u, wp, ps, woa, wop, lg, lb, g1, b1, wr_t)


def _route_kernel(aff_ref, tri_ref, ones_ref, low_ref, pos_ref, roff_ref, thr_ref, *, cap):
    n_e, R, _ = aff_ref.shape
    tri = tri_ref[...]
    ones = ones_ref[...]
    low = low_ref[...]

    def excl_prefix(flag):
        fb = flag.astype(jnp.bfloat16)
        incl = jnp.dot(fb, tri, preferred_element_type=jnp.float32)
        tot = jnp.dot(fb, ones, preferred_element_type=jnp.float32)
        before = jnp.dot(low, tot.astype(jnp.bfloat16), preferred_element_type=jnp.float32)
        return before + incl - flag, before

    for e in range(n_e):
        thr_ref[e] = 0

    def bit_step(it, carry):
        bit = 30 - it
        for e in range(n_e):
            keys = pltpu.bitcast(aff_ref[e], jnp.int32)
            cand = thr_ref[e] | lax.shift_left(jnp.int32(1), bit)
            cnt = jnp.sum(jnp.where(keys >= cand, 1.0, 0.0))
            thr_ref[e] = jnp.where(cnt >= float(cap), cand, thr_ref[e])
        return carry

    lax.fori_loop(0, 31, bit_step, 0)

    for e in range(n_e):
        keys = pltpu.bitcast(aff_ref[e], jnp.int32)
        thr = thr_ref[e]
        gt = jnp.where(keys > thr, 1.0, 0.0)
        eq = jnp.where(keys == thr, 1.0, 0.0)
        need = float(cap) - jnp.sum(gt)
        eq_rank, _ = excl_prefix(eq)
        sel = gt + eq * jnp.where(eq_rank < need, 1.0, 0.0)
        pos, before = excl_prefix(sel)
        pos_ref[e] = jnp.where(sel > 0.5, pos, -1.0).astype(jnp.int32)
        roff_ref[e] = before.astype(jnp.int32)


def _route(aff3, cap):
    n_e, R, _ = aff3.shape
    tri = jnp.asarray(np.triu(np.ones((LANES, LANES), np.float32)), jnp.bfloat16)
    ones = jnp.ones((LANES, LANES), jnp.bfloat16)
    low = jnp.asarray(np.tril(np.ones((R, R), np.float32), -1), jnp.bfloat16)
    return pl.pallas_call(
        functools.partial(_route_kernel, cap=cap),
        out_shape=[jax.ShapeDtypeStruct((n_e, R, LANES), jnp.int32),
                   jax.ShapeDtypeStruct((n_e, R, LANES), jnp.int32)],
        scratch_shapes=[pltpu.SMEM((n_e,), jnp.int32)],
        compiler_params=pltpu.CompilerParams(vmem_limit_bytes=V7X_VMEM_LIMIT),
        name="route",
    )(aff3, tri, ones, low)


def _slot_base(off_ref, e, j, p, nt1):
    a = off_ref[e * nt1 + j]
    a16 = lax.shift_left(lax.shift_right_logical(a, 4), 4)
    return a16 + p * SLOT_WIN


def _window_onehot(pos_ref, rel_bases, min_pos, tile, value_rows=None):
    rows = lax.broadcasted_iota(jnp.int32, (SLOT_WIN, tile), 0)
    blocks = []
    for e in range(N_EXPERTS):
        pos = pos_ref[e:e + 1, :]
        rel = jnp.where(pos >= min_pos[e], pos - rel_bases[e], -1)
        hit = rows == rel
        if value_rows is None:
            blocks.append(jnp.where(hit, 1.0, 0.0).astype(jnp.bfloat16))
        else:
            blocks.append(jnp.where(hit, value_rows[e:e + 1, :], 0.0).astype(jnp.bfloat16))
    return jnp.concatenate(blocks, axis=0)


def _gather_kernel(off_ref, np_ref, pos_ref, h_ref, xe_ref, stage_ref, carry_ref, par_ref, sem,
                   *, nt, cap):
    j = pl.program_id(0)
    nt1 = nt + 1
    tile = h_ref.shape[0]

    def flush_copy(par, e, base):
        return pltpu.make_async_copy(stage_ref.at[par, e], xe_ref.at[e, pl.ds(base, SLOT_WIN)],
                                     sem.at[e])

    @pl.when(j == 0)
    def _():
        carry_ref[...] = jnp.zeros(carry_ref.shape, jnp.bfloat16)
        par_ref[0] = 0
        stage_ref[1] = jnp.zeros(stage_ref.shape[1:], jnp.bfloat16)
        for e in range(N_EXPERTS):
            flush_copy(1, e, cap).start()

    hb = h_ref[...].astype(jnp.bfloat16)

    def one_pass(p, carry):
        par = par_ref[0]
        bases = [_slot_base(off_ref, e, j, p, nt1) for e in range(N_EXPERTS)]
        onehot = _window_onehot(pos_ref, bases, bases, tile)
        rows = jnp.dot(onehot, hb, preferred_element_type=jnp.float32).astype(jnp.bfloat16)
        stage_ref[par] = rows.reshape(N_EXPERTS, SLOT_WIN, D_MODEL)
        for e in range(N_EXPERTS):
            flush_copy(1 - par, e, 0).wait()
        for e in range(N_EXPERTS):
            b = off_ref[e * nt1 + j + 1]
            active = bases[e] <= b
            carry_in = jnp.where(p == 0, carry_ref[e], jnp.zeros_like(carry_ref[e]))
            stage_ref[par, e, 0:BF16_SUBLANES, :] = stage_ref[par, e, 0:BF16_SUBLANES, :] + carry_in
            dst = jnp.where(active, bases[e], cap)
            flush_copy(par, e, pl.multiple_of(dst, BF16_SUBLANES)).start()
            last = jnp.logical_and(active, b < bases[e] + SLOT_WIN)
            g = lax.shift_left(lax.shift_right_logical(b, 4), 4) - bases[e]
            g = jnp.clip(g, 0, SLOT_WIN - BF16_SUBLANES)
            tail = stage_ref[par, e, pl.ds(pl.multiple_of(g, BF16_SUBLANES), BF16_SUBLANES), :]
            carry_ref[e] = jnp.where(last, tail, carry_ref[e])
        par_ref[0] = 1 - par
        return carry

    lax.fori_loop(0, np_ref[j], one_pass, 0)

    @pl.when(j == nt - 1)
    def _():
        par = par_ref[0]
        for e in range(N_EXPERTS):
            flush_copy(1 - par, e, 0).wait()
        stage_ref[par] = jnp.zeros(stage_ref.shape[1:], jnp.bfloat16)
        for e in range(N_EXPERTS):
            flush_copy(par, e, cap).start()
        for e in range(N_EXPERTS):
            flush_copy(par, e, cap).wait()


def _gather(off, npass, pos, h, cap):
    N = h.shape[0]
    nt = N // TOK_TILE
    return pl.pallas_call(
        functools.partial(_gather_kernel, nt=nt, cap=cap),
        grid_spec=pltpu.PrefetchScalarGridSpec(
            num_scalar_prefetch=2,
            grid=(nt,),
            in_specs=[
                pl.BlockSpec((N_EXPERTS, TOK_TILE), lambda j, *_: (0, j)),
                pl.BlockSpec((TOK_TILE, D_MODEL), lambda j, *_: (j, 0)),
            ],
            out_specs=pl.BlockSpec(memory_space=pl.ANY),
            scratch_shapes=[
                pltpu.VMEM((2, N_EXPERTS, SLOT_WIN, D_MODEL), jnp.bfloat16),
                pltpu.VMEM((N_EXPERTS, BF16_SUBLANES, D_MODEL), jnp.bfloat16),
                pltpu.SMEM((1,), jnp.int32),
                pltpu.SemaphoreType.DMA((N_EXPERTS,)),
            ],
        ),
        out_shape=jax.ShapeDtypeStruct((N_EXPERTS, cap + SLOT_WIN, D_MODEL), jnp.bfloat16),
        compiler_params=_cp(("arbitrary",)),
        name="gather",
    )(off, npass, pos, h)


def _ffn_kernel(x_ref, wg_ref, wu_ref, wd_ref, y_ref):
    x = x_ref[0]
    half = D_FF // 2
    acc = None
    for c in range(2):
        cols = slice(c * half, (c + 1) * half)
        g = jnp.dot(x, wg_ref[0, :, cols], preferred_element_type=jnp.float32)
        u = jnp.dot(x, wu_ref[0, :, cols], preferred_element_type=jnp.float32)
        hid = (g * jax.nn.sigmoid(g) * u).astype(jnp.bfloat16)
        part = jnp.dot(hid, wd_ref[0, cols, :], preferred_element_type=jnp.float32)
        acc = part if acc is None else acc + part
    y_ref[0] = acc.astype(jnp.bfloat16)


def _ffn(xe, wg, wu, wd, cap, tile):
    n_e = xe.shape[0]
    return pl.pallas_call(
        _ffn_kernel,
        grid=(n_e, cap // tile),
        in_specs=[
            pl.BlockSpec((1, tile, D_MODEL), lambda e, s: (e, s, 0)),
            pl.BlockSpec((1, D_MODEL, D_FF), lambda e, s: (e, 0, 0)),
            pl.BlockSpec((1, D_MODEL, D_FF), lambda e, s: (e, 0, 0)),
            pl.BlockSpec((1, D_FF, D_MODEL), lambda e, s: (e, 0, 0)),
        ],
        out_specs=pl.BlockSpec((1, tile, D_MODEL), lambda e, s: (e, s, 0)),
        out_shape=jax.ShapeDtypeStruct((n_e, cap, D_MODEL), jnp.bfloat16),
        compiler_params=_cp(("parallel", "arbitrary")),
        name="ffn",
    )(xe, wg, wu, wd)


def _combine_kernel(off_ref, np_ref, pos_ref, aff_ref, h_ref, g2_ref, b2_ref, ye_ref, y_ref,
                    win_ref, slot_ref, sem, *, nt, cap):
    j = pl.program_id(0)
    nt1 = nt + 1
    tile = h_ref.shape[0]

    def fetch_copy(e, base, slot):
        return pltpu.make_async_copy(ye_ref.at[e, pl.ds(base, SLOT_WIN)], win_ref.at[slot, e],
                                     sem.at[slot])

    def read_base(e, jj, p):
        return jnp.minimum(_slot_base(off_ref, e, jj, p, nt1), cap - SLOT_WIN)

    def start_fetch(jj, p, slot):
        for e in range(N_EXPERTS):
            fetch_copy(e, pl.multiple_of(read_base(e, jj, p), BF16_SUBLANES), slot).start()

    def wait_fetch(slot):
        for e in range(N_EXPERTS):
            fetch_copy(e, 0, slot).wait()

    @pl.when(j == 0)
    def _():
        slot_ref[0] = 0
        start_fetch(0, 0, 0)

    n_pass = np_ref[j]
    gates = aff_ref[...]

    def one_pass(p):
        slot = slot_ref[0]
        wait_fetch(slot)

        @pl.when(p + 1 < n_pass)
        def _():
            start_fetch(j, p + 1, 1 - slot)

        @pl.when(jnp.logical_and(p + 1 >= n_pass, j + 1 < nt))
        def _():
            start_fetch(j + 1, 0, 1 - slot)

        onehot = _window_onehot(pos_ref,
                                [read_base(e, j, p) for e in range(N_EXPERTS)],
                                [_slot_base(off_ref, e, j, p, nt1) for e in range(N_EXPERTS)],
                                tile, value_rows=gates)
        rows = win_ref[slot].reshape(N_EXPERTS * SLOT_WIN, D_MODEL)
        slot_ref[0] = 1 - slot
        return lax.dot_general(onehot, rows, (((0,), (0,)), ((), ())),
                               preferred_element_type=jnp.float32)

    ff = lax.fori_loop(1, n_pass, lambda p, ff: ff + one_pass(p), one_pass(0))
    y_ref[...] = _ln(ALPHA * h_ref[...] + ff, g2_ref[...], b2_ref[...])


def _combine(off, npass, pos, aff, h, g2, b2, ye, cap):
    N = h.shape[0]
    nt = N // TOK_TILE
    return pl.pallas_call(
        functools.partial(_combine_kernel, nt=nt, cap=cap),
        grid_spec=pltpu.PrefetchScalarGridSpec(
            num_scalar_prefetch=2,
            grid=(nt,),
            in_specs=[
                pl.BlockSpec((N_EXPERTS, TOK_TILE), lambda j, *_: (0, j)),
                pl.BlockSpec((N_EXPERTS, TOK_TILE), lambda j, *_: (0, j)),
                pl.BlockSpec((TOK_TILE, D_MODEL), lambda j, *_: (j, 0)),
                pl.BlockSpec((1, D_MODEL), lambda j, *_: (0, 0)),
                pl.BlockSpec((1, D_MODEL), lambda j, *_: (0, 0)),
                pl.BlockSpec(memory_space=pl.ANY),
            ],
            out_specs=pl.BlockSpec((TOK_TILE, D_MODEL), lambda j, *_: (j, 0)),
            scratch_shapes=[
                pltpu.VMEM((2, N_EXPERTS, SLOT_WIN, D_MODEL), jnp.bfloat16),
                pltpu.SMEM((1,), jnp.int32),
                pltpu.SemaphoreType.DMA((2,)),
            ],
        ),
        out_shape=jax.ShapeDtypeStruct((N, D_MODEL), jnp.float32),
        compiler_params=_cp(("arbitrary",)),
        name="combine",
    )(off, npass, pos, aff, h, g2, b2, ye)


def _rope_tables(seq_len):
    rows = seq_len // GRID_W
    row = jnp.repeat(jnp.arange(rows), GRID_W).astype(jnp.float32)
    col = jnp.tile(jnp.arange(GRID_W), rows).astype(jnp.float32)
    inv_freq = ROPE_THETA ** (-jnp.arange(ROT_PAIRS, dtype=jnp.float32) / ROT_PAIRS)
    ang_r = inv_freq[:, None] * row[None, :]
    ang_c = inv_freq[:, None] * col[None, :]
    ct = jnp.concatenate([jnp.cos(ang_r), jnp.cos(ang_r), jnp.cos(ang_c), jnp.cos(ang_c)], axis=0)
    st = jnp.concatenate([-jnp.sin(ang_r), jnp.sin(ang_r), -jnp.sin(ang_c), jnp.sin(ang_c)], axis=0)
    return ct, st


def _trunk(x, prm, *, proj_tile, attn_tq, attn_tk, mix_tile, ffn_tile):
    B, S, _ = x.shape
    N = B * S
    cap = (CAPACITY_FACTOR * N) // N_EXPERTS
    assert S % proj_tile == 0 and S % attn_tq == 0 and S % attn_tk == 0 and S % mix_tile == 0
    assert N % TOK_TILE == 0 and TOK_TILE % LANES == 0
    assert cap % BF16_SUBLANES == 0 and cap >= SLOT_WIN and cap % ffn_tile == 0

    ct, st = _rope_tables(S)
    qt, k, vt, u = _in_proj(x, prm["ln_in_g"], prm["ln_in_b"], prm["wt"], prm["wu"], ct, st,
                            prm["gq"], prm["gk"], proj_tile)
    attn_t = _attention(qt, k, vt, attn_tq, attn_tk)
    h, aff = _mix_out(x, attn_t, u, prm["w_pool"], prm["pool_scale"],
                      prm["wo_attn"], prm["wo_pool"], prm["ln_in_g"], prm["ln_in_b"],
                      prm["ln1_g"], prm["ln1_b"], prm["wr_t"], mix_tile)
    h = h.reshape(N, D_MODEL)

    pos3, roff3 = _route(aff.reshape(N_EXPERTS, N // LANES, LANES), cap)
    pos = pos3.reshape(N_EXPERTS, N)
    off = jnp.concatenate([roff3[:, ::TOK_TILE // LANES, 0],
                           jnp.full((N_EXPERTS, 1), cap, jnp.int32)], axis=1)
    a, b = off[:, :-1], off[:, 1:]
    a16 = (a // BF16_SUBLANES) * BF16_SUBLANES
    npass = jnp.max((b - a16) // SLOT_WIN + 1, axis=0).astype(jnp.int32)
    off = off.reshape(-1)

    xe = _gather(off, npass, pos, h, cap)
    ye = _ffn(xe, prm["w_gate"], prm["w_up"], prm["w_down"], cap, ffn_tile)
    y = _combine(off, npass, pos, aff, h, prm["ln2_g"], prm["ln2_b"], ye, cap)
    return y.reshape(B, S, D_MODEL)


def _prepare(ln_in_g, ln_in_b, w_in, q_norm_g, k_norm_g, w_pool, pool_scale, w_out, ln1_g, ln1_b,
             w_router, w_gate, w_up, w_down, ln2_g, ln2_b):
    bf = jnp.bfloat16
    row = lambda v: v.reshape(1, -1).astype(jnp.float32)
    qkv = ATTN_WIDTH + 2 * KV_WIDTH
    return {
        "ln_in_g": row(ln_in_g), "ln_in_b": row(ln_in_b),
        "wt": jnp.transpose(w_in[0][:, :qkv]).astype(bf),
        "wu": w_in[0][:, qkv:].astype(bf),
        "gq": q_norm_g[0].reshape(HEAD_DIM, 1), "gk": k_norm_g[0].reshape(HEAD_DIM, 1),
        "w_pool": w_pool[0].astype(bf), "pool_scale": row(pool_scale[0]),
        "wo_attn": w_out[0][:ATTN_WIDTH].astype(bf), "wo_pool": w_out[0][ATTN_WIDTH:].astype(bf),
        "ln1_g": row(ln1_g[0]), "ln1_b": row(ln1_b[0]),
        "wr_t": jnp.transpose(w_router[0]).astype(bf),
        "w_gate": w_gate[0].astype(bf), "w_up": w_up[0].astype(bf), "w_down": w_down[0].astype(bf),
        "ln2_g": row(ln2_g[0]), "ln2_b": row(ln2_b[0]),
    }


def kernel(x_prompt, x_sample, ln_in_g, ln_in_b, w_in, q_norm_g, k_norm_g, w_pool, pool_scale, w_out,
           ln1_g, ln1_b, w_router, w_gate, w_up, w_down, ln2_g, ln2_b):
    prm = _prepare(ln_in_g, ln_in_b, w_in, q_norm_g, k_norm_g, w_pool, pool_scale, w_out, ln1_g, ln1_b,
                   w_router, w_gate, w_up, w_down, ln2_g, ln2_b)
    tiles = dict(proj_tile=512, attn_tq=512, attn_tk=1024, mix_tile=256, ffn_tile=512)
    return (_trunk(x_prompt, prm, **tiles), _trunk(x_sample, prm, **tiles))
```

```python
import functools
import math

import jax
import jax.numpy as jnp
import numpy as np
from jax import lax
from jax.experimental import pallas as pl
from jax.experimental.pallas import tpu as pltpu

D_MODEL = 1024
ATTN_WIDTH = 512
POOL_WIDTH = 512
HEAD_DIM = 64
N_HEADS = 8
N_KV_HEADS = 2
GQA_GROUP = N_HEADS // N_KV_HEADS
KV_WIDTH = N_KV_HEADS * HEAD_DIM
ROT_PAIRS = HEAD_DIM // 4
ROPE_THETA = 10000.0
GRID_W = 64
POOL_WINDOWS = (2, 4, 8, 16)
POOL_GROUP_W = 128
N_EXPERTS = 16
CAPACITY_FACTOR = 2
D_FF = 2048
LN_EPS = 1e-5
RMS_EPS = 1e-6
ALPHA = 2.0 ** 0.25

LANES = 128
BF16_SUBLANES = 16
V7X_VMEM_LIMIT = 56 * 1024 * 1024

HALO = 16
V_ROWS = 80
SLOT_WIN = 64
TOK_TILE = 256


def _ln(x, g, b):
    mu = jnp.mean(x, axis=-1, keepdims=True)
    xc = x - mu
    var = jnp.mean(xc * xc, axis=-1, keepdims=True)
    return xc * lax.rsqrt(var + LN_EPS) * g + b


def _cp(sem, vmem=V7X_VMEM_LIMIT, flags=None):
    return pltpu.CompilerParams(dimension_semantics=sem, vmem_limit_bytes=vmem, flags=flags)


def _in_proj_kernel(x_ref, g_ref, b_ref, wt_ref, wu_ref, ct_ref, st_ref, gq_ref, gk_ref,
                    q_ref, k_ref, v_ref, u_ref, *, q_scale):
    x0 = _ln(x_ref[0], g_ref[...], b_ref[...])
    xb = x0.astype(jnp.bfloat16)
    pt = lax.dot_general(wt_ref[...], xb, (((1,), (1,)), ((), ())),
                         preferred_element_type=jnp.float32)
    u_ref[0] = jnp.dot(xb, wu_ref[...], preferred_element_type=jnp.float32)
    ct = ct_ref[...]
    st = st_ref[...]

    def norm_rope(xh, gcol):
        ms = jnp.mean(xh * xh, axis=0, keepdims=True)
        xn = xh * lax.rsqrt(ms + RMS_EPS) * gcol
        partner = jnp.concatenate([xn[16:32], xn[0:16], xn[48:64], xn[32:48]], axis=0)
        return xn * ct + partner * st

    gq = gq_ref[...]
    gk = gk_ref[...]
    for h in range(N_HEADS):
        qh = norm_rope(pt[h * HEAD_DIM:(h + 1) * HEAD_DIM], gq) * q_scale
        q_ref[0, h * HEAD_DIM:(h + 1) * HEAD_DIM, :] = qh.astype(jnp.bfloat16)
    kt = jnp.concatenate(
        [norm_rope(pt[ATTN_WIDTH + h * HEAD_DIM:ATTN_WIDTH + (h + 1) * HEAD_DIM], gk)
         for h in range(N_KV_HEADS)], axis=0)
    k_ref[0] = jnp.transpose(kt).astype(jnp.bfloat16)
    ones = jnp.ones((V_ROWS - HEAD_DIM, pt.shape[1]), jnp.bfloat16)
    for h in range(N_KV_HEADS):
        r0 = ATTN_WIDTH + KV_WIDTH + h * HEAD_DIM
        v_ref[0, h, 0:HEAD_DIM, :] = pt[r0:r0 + HEAD_DIM].astype(jnp.bfloat16)
        v_ref[0, h, HEAD_DIM:V_ROWS, :] = ones


def _in_proj(x, ln_g, ln_b, wt, wu, ct, st, gq, gk, tile):
    B, S, _ = x.shape
    q_scale = (HEAD_DIM ** -0.5) * math.log2(math.e)
    const = lambda shape: pl.BlockSpec(shape, lambda b, i: tuple(0 for _ in shape))
    return pl.pallas_call(
        functools.partial(_in_proj_kernel, q_scale=q_scale),
        grid=(B, S // tile),
        in_specs=[
            pl.BlockSpec((1, tile, D_MODEL), lambda b, i: (b, i, 0)),
            const((1, D_MODEL)), const((1, D_MODEL)),
            const((ATTN_WIDTH + 2 * KV_WIDTH, D_MODEL)),
            const((D_MODEL, POOL_WIDTH)),
            pl.BlockSpec((HEAD_DIM, tile), lambda b, i: (0, i)),
            pl.BlockSpec((HEAD_DIM, tile), lambda b, i: (0, i)),
            const((HEAD_DIM, 1)), const((HEAD_DIM, 1)),
        ],
        out_specs=[
            pl.BlockSpec((1, ATTN_WIDTH, tile), lambda b, i: (b, 0, i)),
            pl.BlockSpec((1, tile, KV_WIDTH), lambda b, i: (b, i, 0)),
            pl.BlockSpec((1, N_KV_HEADS, V_ROWS, tile), lambda b, i: (b, 0, 0, i)),
            pl.BlockSpec((1, tile, POOL_WIDTH), lambda b, i: (b, i, 0)),
        ],
        out_shape=[
            jax.ShapeDtypeStruct((B, ATTN_WIDTH, S), jnp.bfloat16),
            jax.ShapeDtypeStruct((B, S, KV_WIDTH), jnp.bfloat16),
            jax.ShapeDtypeStruct((B, N_KV_HEADS, V_ROWS, S), jnp.bfloat16),
            jax.ShapeDtypeStruct((B, S, POOL_WIDTH), jnp.float32),
        ],
        compiler_params=_cp(("parallel", "parallel")),
        name="in_proj",
    )(x, ln_g, ln_b, wt, wu, ct, st, gq, gk)


def _attn_kernel(q_ref, k_ref, v_ref, o_ref, qp_ref, m_ref, acc_ref, s_ref, cmax_ref, *, tk):
    S = k_ref.shape[1]
    tq = q_ref.shape[2]
    n_chunks = S // tk
    zeros = jnp.zeros((HEAD_DIM, tq), jnp.bfloat16)
    for h in range(N_HEADS):
        qh = q_ref[0, h * HEAD_DIM:(h + 1) * HEAD_DIM, :]
        qp_ref[h] = jnp.concatenate([qh, zeros] if h < GQA_GROUP else [zeros, qh], axis=0)
    m_ref[...] = jnp.full(m_ref.shape, -jnp.inf, jnp.float32)
    acc_ref[...] = jnp.zeros(acc_ref.shape, jnp.float32)

    def scores(c, h):
        c0 = pl.multiple_of(c * tk, tk)
        s = jnp.dot(k_ref[0, pl.ds(c0, tk), :], qp_ref[h],
                    preferred_element_type=jnp.float32)
        s_ref[h % 2] = s
        cmax_ref[h % 2] = jnp.max(s, axis=0, keepdims=True)

    scores(0, 0)

    def chunk(c, carry):
        c0 = pl.multiple_of(c * tk, tk)
        for h in range(N_HEADS):
            m_old = m_ref[h]
            m_new = jnp.maximum(m_old, cmax_ref[h % 2])
            alpha = jnp.exp2(m_old - m_new)
            p = jnp.exp2(s_ref[h % 2] - m_new).astype(jnp.bfloat16)
            if h + 1 < N_HEADS:
                scores(c, h + 1)
            else:
                scores(jnp.minimum(c + 1, n_chunks - 1), 0)
            vc = v_ref[0, h // GQA_GROUP, :, pl.ds(c0, tk)]
            acc_ref[h] = alpha * acc_ref[h] + jnp.dot(vc, p, preferred_element_type=jnp.float32)
            m_ref[h] = m_new
        return carry

    lax.fori_loop(0, n_chunks, chunk, 0)
    for h in range(N_HEADS):
        a = acc_ref[h]
        o_ref[0, h * HEAD_DIM:(h + 1) * HEAD_DIM, :] = (
            a[0:HEAD_DIM] / a[HEAD_DIM:HEAD_DIM + 1]).astype(jnp.bfloat16)


def _attention(qt, k, vt, tq, tk):
    B, _, S = qt.shape
    return pl.pallas_call(
        functools.partial(_attn_kernel, tk=tk),
        grid=(B, S // tq),
        in_specs=[
            pl.BlockSpec((1, ATTN_WIDTH, tq), lambda b, i: (b, 0, i)),
            pl.BlockSpec((1, S, KV_WIDTH), lambda b, i: (b, 0, 0)),
            pl.BlockSpec((1, N_KV_HEADS, V_ROWS, S), lambda b, i: (b, 0, 0, 0)),
        ],
        out_specs=pl.BlockSpec((1, ATTN_WIDTH, tq), lambda b, i: (b, 0, i)),
        out_shape=jax.ShapeDtypeStruct((B, ATTN_WIDTH, S), jnp.bfloat16),
        scratch_shapes=[
            pltpu.VMEM((N_HEADS, KV_WIDTH, tq), jnp.bfloat16),
            pltpu.VMEM((N_HEADS, 1, tq), jnp.float32),
            pltpu.VMEM((N_HEADS, V_ROWS, tq), jnp.float32),
            pltpu.VMEM((2, tk, tq), jnp.float32),
            pltpu.VMEM((2, 1, tq), jnp.float32),
        ],
        compiler_params=_cp(("parallel", "parallel")),
        name="attention",
    )(qt, k, vt)


def _mix_kernel(x_ref, at_ref, uc_ref, up_ref, un_ref, wp_ref, ps_ref, woa_ref, wop_ref,
                lg_ref, lb_ref, g1_ref, b1_ref, wr_ref, h_ref, aff_ref, ue_ref, *, seq_len):
    i = pl.program_id(1)
    T = uc_ref.shape[1]
    n_tiles = seq_len // T
    ue_ref[0:HALO, :] = jnp.where(i > 0, up_ref[0], 0.0)
    ue_ref[HALO:HALO + T, :] = uc_ref[0]
    ue_ref[HALO + T:, :] = jnp.where(i < n_tiles - 1, un_ref[0], 0.0)
    tg = (i * T + lax.broadcasted_iota(jnp.int32, (T, 1), 0))
    pooled = []
    for g, w in enumerate(POOL_WINDOWS):
        cols = slice(g * POOL_GROUP_W, (g + 1) * POOL_GROUP_W)
        win = ue_ref[HALO - w // 2:HALO - w // 2 + T, cols]
        for d in range(1 - w // 2, w - w // 2):
            win = win + ue_ref[HALO + d:HALO + d + T, cols]
        lo = jnp.maximum(tg - w // 2, 0)
        hi = jnp.minimum(tg - w // 2 + w, seq_len)
        cnt = (hi - lo).astype(jnp.float32)
        m = win / cnt - uc_ref[0, :, cols]
        y = jnp.dot(m.astype(jnp.bfloat16), wp_ref[g], preferred_element_type=jnp.float32)
        pooled.append(y * ps_ref[:, cols])
    pool = jnp.concatenate(pooled, axis=1).astype(jnp.bfloat16)
    mix = lax.dot_general(at_ref[0], woa_ref[...], (((0,), (0,)), ((), ())),
                          preferred_element_type=jnp.float32)
    mix = mix + jnp.dot(pool, wop_ref[...], preferred_element_type=jnp.float32)
    x0 = _ln(x_ref[0], lg_ref[...], lb_ref[...])
    h = _ln(ALPHA * x0 + mix, g1_ref[...], b1_ref[...])
    h_ref[0] = h
    logits = lax.dot_general(wr_ref[...], h.astype(jnp.bfloat16), (((1,), (1,)), ((), ())),
                             preferred_element_type=jnp.float32)
    e = jnp.exp(logits - jnp.max(logits, axis=0, keepdims=True))
    aff_ref[...] = e / jnp.sum(e, axis=0, keepdims=True)


def _mix_out(x, attn_t, u, wp, ps, woa, wop, lg, lb, g1, b1, wr_t, tile):
    B, S, _ = x.shape
    nt = S // tile
    hb = tile // HALO
    const = lambda shape: pl.BlockSpec(shape, lambda b, i: tuple(0 for _ in shape))
    return pl.pallas_call(
        functools.partial(_mix_kernel, seq_len=S),
        grid=(B, nt),
        in_specs=[
            pl.BlockSpec((1, tile, D_MODEL), lambda b, i: (b, i, 0)),
            pl.BlockSpec((1, ATTN_WIDTH, tile), lambda b, i: (b, 0, i)),
            pl.BlockSpec((1, tile, POOL_WIDTH), lambda b, i: (b, i, 0)),
            pl.BlockSpec((1, HALO, POOL_WIDTH), lambda b, i: (b, jnp.maximum(i * hb - 1, 0), 0)),
            pl.BlockSpec((1, HALO, POOL_WIDTH),
                         lambda b, i: (b, jnp.minimum((i + 1) * hb, S // HALO - 1), 0)),
            const(wp.shape), const((1, POOL_WIDTH)),
            const((ATTN_WIDTH, D_MODEL)), const((POOL_WIDTH, D_MODEL)),
            const((1, D_MODEL)), const((1, D_MODEL)), const((1, D_MODEL)), const((1, D_MODEL)),
            const((N_EXPERTS, D_MODEL)),
        ],
        out_specs=[
            pl.BlockSpec((1, tile, D_MODEL), lambda b, i: (b, i, 0)),
            pl.BlockSpec((N_EXPERTS, tile), lambda b, i: (0, b * nt + i)),
        ],
        out_shape=[
            jax.ShapeDtypeStruct((B, S, D_MODEL), jnp.float32),
            jax.ShapeDtypeStruct((N_EXPERTS, B * S), jnp.float32),
        ],
        scratch_shapes=[pltpu.VMEM((tile + 2 * HALO, POOL_WIDTH), jnp.float32)],
        compiler_params=_cp(("parallel", "parallel")),
        name="mix_out",
    )(x, attn_t, u, u, u, wp, ps, woa, wop, lg, lb, g1, b1, wr_t)


def _route_kernel(aff_ref, tri_ref, ones_ref, low_ref, pos_ref, roff_ref, thr_ref, *, cap):
    n_e, R, _ = aff_ref.shape
    tri = tri_ref[...]
    ones = ones_ref[...]
    low = low_ref[...]

    def excl_prefix(flag):
        fb = flag.astype(jnp.bfloat16)
        incl = jnp.dot(fb, tri, preferred_element_type=jnp.float32)
        tot = jnp.dot(fb, ones, preferred_element_type=jnp.float32)
        before = jnp.dot(low, tot.astype(jnp.bfloat16), preferred_element_type=jnp.float32)
        return before + incl - flag, before

    for e in range(n_e):
        thr_ref[e] = 0

    def bit_step(it, carry):
        bit = 30 - it
        for e in range(n_e):
            keys = pltpu.bitcast(aff_ref[e], jnp.int32)
            cand = thr_ref[e] | lax.shift_left(jnp.int32(1), bit)
            cnt = jnp.sum(jnp.where(keys >= cand, 1.0, 0.0))
            thr_ref[e] = jnp.where(cnt >= float(cap), cand, thr_ref[e])
        return carry

    lax.fori_loop(0, 31, bit_step, 0)

    for e in range(n_e):
        keys = pltpu.bitcast(aff_ref[e], jnp.int32)
        thr = thr_ref[e]
        gt = jnp.where(keys > thr, 1.0, 0.0)
        eq = jnp.where(keys == thr, 1.0, 0.0)
        need = float(cap) - jnp.sum(gt)
        eq_rank, _ = excl_prefix(eq)
        sel = gt + eq * jnp.where(eq_rank < need, 1.0, 0.0)
        pos, before = excl_prefix(sel)
        pos_ref[e] = jnp.where(sel > 0.5, pos, -1.0).astype(jnp.int32)
        roff_ref[e] = before.astype(jnp.int32)


def _route(aff3, cap):
    n_e, R, _ = aff3.shape
    tri = jnp.asarray(np.triu(np.ones((LANES, LANES), np.float32)), jnp.bfloat16)
    ones = jnp.ones((LANES, LANES), jnp.bfloat16)
    low = jnp.asarray(np.tril(np.ones((R, R), np.float32), -1), jnp.bfloat16)
    return pl.pallas_call(
        functools.partial(_route_kernel, cap=cap),
        out_shape=[jax.ShapeDtypeStruct((n_e, R, LANES), jnp.int32),
                   jax.ShapeDtypeStruct((n_e, R, LANES), jnp.int32)],
        scratch_shapes=[pltpu.SMEM((n_e,), jnp.int32)],
        compiler_params=pltpu.CompilerParams(vmem_limit_bytes=V7X_VMEM_LIMIT),
        name="route",
    )(aff3, tri, ones, low)


def _slot_base(off_ref, e, j, p, nt1):
    a = off_ref[e * nt1 + j]
    a16 = lax.shift_left(lax.shift_right_logical(a, 4), 4)
    return a16 + p * SLOT_WIN


def _window_onehot(pos_ref, rel_bases, min_pos, tile, value_rows=None):
    rows = lax.broadcasted_iota(jnp.int32, (SLOT_WIN, tile), 0)
    blocks = []
    for e in range(N_EXPERTS):
        pos = pos_ref[e:e + 1, :]
        rel = jnp.where(pos >= min_pos[e], pos - rel_bases[e], -1)
        hit = rows == rel
        if value_rows is None:
            blocks.append(jnp.where(hit, 1.0, 0.0).astype(jnp.bfloat16))
        else:
            blocks.append(jnp.where(hit, value_rows[e:e + 1, :], 0.0).astype(jnp.bfloat16))
    return jnp.concatenate(blocks, axis=0)


def _gather_kernel(off_ref, np_ref, pos_ref, h_ref, xe_ref, stage_ref, carry_ref, par_ref, sem,
                   *, nt, cap):
    j = pl.program_id(0)
    nt1 = nt + 1
    tile = h_ref.shape[0]

    def flush_copy(par, e, base):
        return pltpu.make_async_copy(stage_ref.at[par, e], xe_ref.at[e, pl.ds(base, SLOT_WIN)],
                                     sem.at[e])

    @pl.when(j == 0)
    def _():
        carry_ref[...] = jnp.zeros(carry_ref.shape, jnp.bfloat16)
        par_ref[0] = 0
        stage_ref[1] = jnp.zeros(stage_ref.shape[1:], jnp.bfloat16)
        for e in range(N_EXPERTS):
            flush_copy(1, e, cap).start()

    hb = h_ref[...].astype(jnp.bfloat16)

    def one_pass(p, carry):
        par = par_ref[0]
        bases = [_slot_base(off_ref, e, j, p, nt1) for e in range(N_EXPERTS)]
        onehot = _window_onehot(pos_ref, bases, bases, tile)
        rows = jnp.dot(onehot, hb, preferred_element_type=jnp.float32).astype(jnp.bfloat16)
        stage_ref[par] = rows.reshape(N_EXPERTS, SLOT_WIN, D_MODEL)
        for e in range(N_EXPERTS):
            flush_copy(1 - par, e, 0).wait()
        for e in range(N_EXPERTS):
            b = off_ref[e * nt1 + j + 1]
            active = bases[e] <= b
            carry_in = jnp.where(p == 0, carry_ref[e], jnp.zeros_like(carry_ref[e]))
            stage_ref[par, e, 0:BF16_SUBLANES, :] = stage_ref[par, e, 0:BF16_SUBLANES, :] + carry_in
            dst = jnp.where(active, bases[e], cap)
            flush_copy(par, e, pl.multiple_of(dst, BF16_SUBLANES)).start()
            last = jnp.logical_and(active, b < bases[e] + SLOT_WIN)
            g = lax.shift_left(lax.shift_right_logical(b, 4), 4) - bases[e]
            g = jnp.clip(g, 0, SLOT_WIN - BF16_SUBLANES)
            tail = stage_ref[par, e, pl.ds(pl.multiple_of(g, BF16_SUBLANES), BF16_SUBLANES), :]
            carry_ref[e] = jnp.where(last, tail, carry_ref[e])
        par_ref[0] = 1 - par
        return carry

    lax.fori_loop(0, np_ref[j], one_pass, 0)

    @pl.when(j == nt - 1)
    def _():
        par = par_ref[0]
        for e in range(N_EXPERTS):
            flush_copy(1 - par, e, 0).wait()
        stage_ref[par] = jnp.zeros(stage_ref.shape[1:], jnp.bfloat16)
        for e in range(N_EXPERTS):
            flush_copy(par, e, cap).start()
        for e in range(N_EXPERTS):
            flush_copy(par, e, cap).wait()


def _gather(off, npass, pos, h, cap):
    N = h.shape[0]
    nt = N // TOK_TILE
    return pl.pallas_call(
        functools.partial(_gather_kernel, nt=nt, cap=cap),
        grid_spec=pltpu.PrefetchScalarGridSpec(
            num_scalar_prefetch=2,
            grid=(nt,),
            in_specs=[
                pl.BlockSpec((N_EXPERTS, TOK_TILE), lambda j, *_: (0, j)),
                pl.BlockSpec((TOK_TILE, D_MODEL), lambda j, *_: (j, 0)),
            ],
            out_specs=pl.BlockSpec(memory_space=pl.ANY),
            scratch_shapes=[
                pltpu.VMEM((2, N_EXPERTS, SLOT_WIN, D_MODEL), jnp.bfloat16),
                pltpu.VMEM((N_EXPERTS, BF16_SUBLANES, D_MODEL), jnp.bfloat16),
                pltpu.SMEM((1,), jnp.int32),
                pltpu.SemaphoreType.DMA((N_EXPERTS,)),
            ],
        ),
        out_shape=jax.ShapeDtypeStruct((N_EXPERTS, cap + SLOT_WIN, D_MODEL), jnp.bfloat16),
        compiler_params=_cp(("arbitrary",)),
        name="gather",
    )(off, npass, pos, h)


def _ffn_kernel(x_ref, wg_ref, wu_ref, wd_ref, y_ref):
    x = x_ref[0]
    half = D_FF // 2
    acc = None
    for c in range(2):
        cols = slice(c * half, (c + 1) * half)
        g = jnp.dot(x, wg_ref[0, :, cols], preferred_element_type=jnp.float32)
        u = jnp.dot(x, wu_ref[0, :, cols], preferred_element_type=jnp.float32)
        hid = (g * jax.nn.sigmoid(g) * u).astype(jnp.bfloat16)
        part = jnp.dot(hid, wd_ref[0, cols, :], preferred_element_type=jnp.float32)
        acc = part if acc is None else acc + part
    y_ref[0] = acc.astype(jnp.bfloat16)


def _ffn(xe, wg, wu, wd, cap, tile):
    n_e = xe.shape[0]
    return pl.pallas_call(
        _ffn_kernel,
        grid=(n_e, cap // tile),
        in_specs=[
            pl.BlockSpec((1, tile, D_MODEL), lambda e, s: (e, s, 0)),
            pl.BlockSpec((1, D_MODEL, D_FF), lambda e, s: (e, 0, 0)),
            pl.BlockSpec((1, D_MODEL, D_FF), lambda e, s: (e, 0, 0)),
            pl.BlockSpec((1, D_FF, D_MODEL), lambda e, s: (e, 0, 0)),
        ],
        out_specs=pl.BlockSpec((1, tile, D_MODEL), lambda e, s: (e, s, 0)),
        out_shape=jax.ShapeDtypeStruct((n_e, cap, D_MODEL), jnp.bfloat16),
        compiler_params=_cp(("parallel", "arbitrary")),
        name="ffn",
    )(xe, wg, wu, wd)


def _combine_kernel(off_ref, np_ref, pos_ref, aff_ref, h_ref, g2_ref, b2_ref, ye_ref, y_ref,
                    win_ref, slot_ref, sem, *, nt, cap):
    j = pl.program_id(0)
    nt1 = nt + 1
    tile = h_ref.shape[0]

    def fetch_copy(e, base, slot):
        return pltpu.make_async_copy(ye_ref.at[e, pl.ds(base, SLOT_WIN)], win_ref.at[slot, e],
                                     sem.at[slot])

    def read_base(e, jj, p):
        return jnp.minimum(_slot_base(off_ref, e, jj, p, nt1), cap - SLOT_WIN)

    def start_fetch(jj, p, slot):
        for e in range(N_EXPERTS):
            fetch_copy(e, pl.multiple_of(read_base(e, jj, p), BF16_SUBLANES), slot).start()

    def wait_fetch(slot):
        for e in range(N_EXPERTS):
            fetch_copy(e, 0, slot).wait()

    @pl.when(j == 0)
    def _():
        slot_ref[0] = 0
        start_fetch(0, 0, 0)

    n_pass = np_ref[j]
    gates = aff_ref[...]

    def one_pass(p):
        slot = slot_ref[0]
        wait_fetch(slot)

        @pl.when(p + 1 < n_pass)
        def _():
            start_fetch(j, p + 1, 1 - slot)

        @pl.when(jnp.logical_and(p + 1 >= n_pass, j + 1 < nt))
        def _():
            start_fetch(j + 1, 0, 1 - slot)

        onehot = _window_onehot(pos_ref,
                                [read_base(e, j, p) for e in range(N_EXPERTS)],
                                [_slot_base(off_ref, e, j, p, nt1) for e in range(N_EXPERTS)],
                                tile, value_rows=gates)
        rows = win_ref[slot].reshape(N_EXPERTS * SLOT_WIN, D_MODEL)
        slot_ref[0] = 1 - slot
        return lax.dot_general(onehot, rows, (((0,), (0,)), ((), ())),
                               preferred_element_type=jnp.float32)

    ff = lax.fori_loop(1, n_pass, lambda p, ff: ff + one_pass(p), one_pass(0))
    y_ref[...] = _ln(ALPHA * h_ref[...] + ff, g2_ref[...], b2_ref[...])


def _combine(off, npass, pos, aff, h, g2, b2, ye, cap):
    N = h.shape[0]
    nt = N // TOK_TILE
    return pl.pallas_call(
        functools.partial(_combine_kernel, nt=nt, cap=cap),
        grid_spec=pltpu.PrefetchScalarGridSpec(
            num_scalar_prefetch=2,
            grid=(nt,),
            in_specs=[
                pl.BlockSpec((N_EXPERTS, TOK_TILE), lambda j, *_: (0, j)),
                pl.BlockSpec((N_EXPERTS, TOK_TILE), lambda j, *_: (0, j)),
                pl.BlockSpec((TOK_TILE, D_MODEL), lambda j, *_: (j, 0)),
                pl.BlockSpec((1, D_MODEL), lambda j, *_: (0, 0)),
                pl.BlockSpec((1, D_MODEL), lambda j, *_: (0, 0)),
                pl.BlockSpec(memory_space=pl.ANY),
            ],
            out_specs=pl.BlockSpec((TOK_TILE, D_MODEL), lambda j, *_: (j, 0)),
            scratch_shapes=[
                pltpu.VMEM((2, N_EXPERTS, SLOT_WIN, D_MODEL), jnp.bfloat16),
                pltpu.SMEM((1,), jnp.int32),
                pltpu.SemaphoreType.DMA((2,)),
            ],
        ),
        out_shape=jax.ShapeDtypeStruct((N, D_MODEL), jnp.float32),
        compiler_params=_cp(("arbitrary",)),
        name="combine",
    )(off, npass, pos, aff, h, g2, b2, ye)


def _rope_tables(seq_len):
    rows = seq_len // GRID_W
    row = jnp.repeat(jnp.arange(rows), GRID_W).astype(jnp.float32)
    col = jnp.tile(jnp.arange(GRID_W), rows).astype(jnp.float32)
    inv_freq = ROPE_THETA ** (-jnp.arange(ROT_PAIRS, dtype=jnp.float32) / ROT_PAIRS)
    ang_r = inv_freq[:, None] * row[None, :]
    ang_c = inv_freq[:, None] * col[None, :]
    ct = jnp.concatenate([jnp.cos(ang_r), jnp.cos(ang_r), jnp.cos(ang_c), jnp.cos(ang_c)], axis=0)
    st = jnp.concatenate([-jnp.sin(ang_r), jnp.sin(ang_r), -jnp.sin(ang_c), jnp.sin(ang_c)], axis=0)
    return ct, st


def _trunk(x, prm, *, proj_tile, attn_tq, attn_tk, mix_tile, ffn_tile):
    B, S, _ = x.shape
    N = B * S
    cap = (CAPACITY_FACTOR * N) // N_EXPERTS
    assert S % proj_tile == 0 and S % attn_tq == 0 and S % attn_tk == 0 and S % mix_tile == 0
    assert N % TOK_TILE == 0 and TOK_TILE % LANES == 0
    assert cap % BF16_SUBLANES == 0 and cap >= SLOT_WIN and cap % ffn_tile == 0

    ct, st = _rope_tables(S)
    qt, k, vt, u = _in_proj(x, prm["ln_in_g"], prm["ln_in_b"], prm["wt"], prm["wu"], ct, st,
                            prm["gq"], prm["gk"], proj_tile)
    attn_t = _attention(qt, k, vt, attn_tq, attn_tk)
    h, aff = _mix_out(x, attn_t, u, prm["w_pool"], prm["pool_scale"],
                      prm["wo_attn"], prm["wo_pool"], prm["ln_in_g"], prm["ln_in_b"],
                      prm["ln1_g"], prm["ln1_b"], prm["wr_t"], mix_tile)
    h = h.reshape(N, D_MODEL)

    pos3, roff3 = _route(aff.reshape(N_EXPERTS, N // LANES, LANES), cap)
    pos = pos3.reshape(N_EXPERTS, N)
    off = jnp.concatenate([roff3[:, ::TOK_TILE // LANES, 0],
                           jnp.full((N_EXPERTS, 1), cap, jnp.int32)], axis=1)
    a, b = off[:, :-1], off[:, 1:]
    a16 = (a // BF16_SUBLANES) * BF16_SUBLANES
    npass = jnp.max((b - a16) // SLOT_WIN + 1, axis=0).astype(jnp.int32)
    off = off.reshape(-1)

    xe = _gather(off, npass, pos, h, cap)
    ye = _ffn(xe, prm["w_gate"], prm["w_up"], prm["w_down"], cap, ffn_tile)
    y = _combine(off, npass, pos, aff, h, prm["ln2_g"], prm["ln2_b"], ye, cap)
    return y.reshape(B, S, D_MODEL)


def _prepare(ln_in_g, ln_in_b, w_in, q_norm_g, k_norm_g, w_pool, pool_scale, w_out, ln1_g, ln1_b,
             w_router, w_gate, w_up, w_down, ln2_g, ln2_b):
    bf = jnp.bfloat16
    row = lambda v: v.reshape(1, -1).astype(jnp.float32)
    qkv = ATTN_WIDTH + 2 * KV_WIDTH
    return {
        "ln_in_g": row(ln_in_g), "ln_in_b": row(ln_in_b),
        "wt": jnp.transpose(w_in[0][:, :qkv]).astype(bf),
        "wu": w_in[0][:, qkv:].astype(bf),
        "gq": q_norm_g[0].reshape(HEAD_DIM, 1), "gk": k_norm_g[0].reshape(HEAD_DIM, 1),
        "w_pool": w_pool[0].astype(bf), "pool_scale": row(pool_scale[0]),
        "wo_attn": w_out[0][:ATTN_WIDTH].astype(bf), "wo_pool": w_out[0][ATTN_WIDTH:].astype(bf),
        "ln1_g": row(ln1_g[0]), "ln1_b": row(ln1_b[0]),
        "wr_t": jnp.transpose(w_router[0]).astype(bf),
        "w_gate": w_gate[0].astype(bf), "w_up": w_up[0].astype(bf), "w_down": w_down[0].astype(bf),
        "ln2_g": row(ln2_g[0]), "ln2_b": row(ln2_b[0]),
    }


def kernel(x_prompt, x_sample, ln_in_g, ln_in_b, w_in, q_norm_g, k_norm_g, w_pool, pool_scale, w_out,
           ln1_g, ln1_b, w_router, w_gate, w_up, w_down, ln2_g, ln2_b):
    prm = _prepare(ln_in_g, ln_in_b, w_in, q_norm_g, k_norm_g, w_pool, pool_scale, w_out, ln1_g, ln1_b,
                   w_router, w_gate, w_up, w_down, ln2_g, ln2_b)
    tiles = dict(proj_tile=1024, attn_tq=512, attn_tk=1024, mix_tile=512, ffn_tile=1024)
    return (_trunk(x_prompt, prm, **tiles), _trunk(x_sample, prm, **tiles))
```

```python
import functools
import math

import jax
import jax.numpy as jnp
import numpy as np
from jax import lax
from jax.experimental import pallas as pl
from jax.experimental.pallas import tpu as pltpu

D_MODEL = 1024
ATTN_WIDTH = 512
POOL_WIDTH = 512
HEAD_DIM = 64
N_HEADS = 8
N_KV_HEADS = 2
GQA_GROUP = N_HEADS // N_KV_HEADS
KV_WIDTH = N_KV_HEADS * HEAD_DIM
ROT_PAIRS = HEAD_DIM // 4
ROPE_THETA = 10000.0
GRID_W = 64
POOL_WINDOWS = (2, 4, 8, 16)
POOL_GROUP_W = 128
N_EXPERTS = 16
CAPACITY_FACTOR = 2
D_FF = 2048
LN_EPS = 1e-5
RMS_EPS = 1e-6
ALPHA = 2.0 ** 0.25

LANES = 128
BF16_SUBLANES = 16
F32_MIN_NORMAL_BITS = 0x00800000
V7X_VMEM_LIMIT = 56 * 1024 * 1024

HALO = 16
V_ROWS = 80
SLOT_WIN = 64
TOK_TILE = 256


def _ln(x, g, b):
    mu = jnp.mean(x, axis=-1, keepdims=True)
    xc = x - mu
    var = jnp.mean(xc * xc, axis=-1, keepdims=True)
    return xc * lax.rsqrt(var + LN_EPS) * g + b


def _cp(sem, vmem=V7X_VMEM_LIMIT, flags=None):
    return pltpu.CompilerParams(dimension_semantics=sem, vmem_limit_bytes=vmem, flags=flags)


def _in_proj_kernel(x_ref, g_ref, b_ref, wt_ref, wu_ref, ct_ref, st_ref, gq_ref, gk_ref,
                    q_ref, k_ref, v_ref, u_ref, *, q_scale):
    x0 = _ln(x_ref[0], g_ref[...], b_ref[...])
    xb = x0.astype(jnp.bfloat16)
    pt = lax.dot_general(wt_ref[...], xb, (((1,), (1,)), ((), ())),
                         preferred_element_type=jnp.float32)
    u_ref[0] = jnp.dot(xb, wu_ref[...], preferred_element_type=jnp.float32)
    ct = ct_ref[...]
    st = st_ref[...]

    def norm_rope(xh, gcol):
        ms = jnp.mean(xh * xh, axis=0, keepdims=True)
        xn = xh * lax.rsqrt(ms + RMS_EPS) * gcol
        partner = jnp.concatenate([xn[16:32], xn[0:16], xn[48:64], xn[32:48]], axis=0)
        return xn * ct + partner * st

    gq = gq_ref[...]
    gk = gk_ref[...]
    for h in range(N_HEADS):
        qh = norm_rope(pt[h * HEAD_DIM:(h + 1) * HEAD_DIM], gq) * q_scale
        q_ref[0, h * HEAD_DIM:(h + 1) * HEAD_DIM, :] = qh.astype(jnp.bfloat16)
    kt = jnp.concatenate(
        [norm_rope(pt[ATTN_WIDTH + h * HEAD_DIM:ATTN_WIDTH + (h + 1) * HEAD_DIM], gk)
         for h in range(N_KV_HEADS)], axis=0)
    k_ref[0] = jnp.transpose(kt).astype(jnp.bfloat16)
    ones = jnp.ones((V_ROWS - HEAD_DIM, pt.shape[1]), jnp.bfloat16)
    for h in range(N_KV_HEADS):
        r0 = ATTN_WIDTH + KV_WIDTH + h * HEAD_DIM
        v_ref[0, h, 0:HEAD_DIM, :] = pt[r0:r0 + HEAD_DIM].astype(jnp.bfloat16)
        v_ref[0, h, HEAD_DIM:V_ROWS, :] = ones


def _in_proj(x, ln_g, ln_b, wt, wu, ct, st, gq, gk, tile):
    B, S, _ = x.shape
    q_scale = (HEAD_DIM ** -0.5) * math.log2(math.e)
    const = lambda shape: pl.BlockSpec(shape, lambda b, i: tuple(0 for _ in shape))
    return pl.pallas_call(
        functools.partial(_in_proj_kernel, q_scale=q_scale),
        grid=(B, S // tile),
        in_specs=[
            pl.BlockSpec((1, tile, D_MODEL), lambda b, i: (b, i, 0)),
            const((1, D_MODEL)), const((1, D_MODEL)),
            const((ATTN_WIDTH + 2 * KV_WIDTH, D_MODEL)),
            const((D_MODEL, POOL_WIDTH)),
            pl.BlockSpec((HEAD_DIM, tile), lambda b, i: (0, i)),
            pl.BlockSpec((HEAD_DIM, tile), lambda b, i: (0, i)),
            const((HEAD_DIM, 1)), const((HEAD_DIM, 1)),
        ],
        out_specs=[
            pl.BlockSpec((1, ATTN_WIDTH, tile), lambda b, i: (b, 0, i)),
            pl.BlockSpec((1, tile, KV_WIDTH), lambda b, i: (b, i, 0)),
            pl.BlockSpec((1, N_KV_HEADS, V_ROWS, tile), lambda b, i: (b, 0, 0, i)),
            pl.BlockSpec((1, tile, POOL_WIDTH), lambda b, i: (b, i, 0)),
        ],
        out_shape=[
            jax.ShapeDtypeStruct((B, ATTN_WIDTH, S), jnp.bfloat16),
            jax.ShapeDtypeStruct((B, S, KV_WIDTH), jnp.bfloat16),
            jax.ShapeDtypeStruct((B, N_KV_HEADS, V_ROWS, S), jnp.bfloat16),
            jax.ShapeDtypeStruct((B, S, POOL_WIDTH), jnp.float32),
        ],
        compiler_params=_cp(("parallel", "parallel")),
        name="in_proj",
    )(x, ln_g, ln_b, wt, wu, ct, st, gq, gk)


def _attn_kernel(q_ref, k_ref, v_ref, o_ref, qp_ref, m_ref, acc_ref, s_ref, cmax_ref, *, tk):
    S = k_ref.shape[1]
    tq = q_ref.shape[2]
    n_chunks = S // tk
    zeros = jnp.zeros((HEAD_DIM, tq), jnp.bfloat16)
    for h in range(N_HEADS):
        qh = q_ref[0, h * HEAD_DIM:(h + 1) * HEAD_DIM, :]
        qp_ref[h] = jnp.concatenate([qh, zeros] if h < GQA_GROUP else [zeros, qh], axis=0)
    m_ref[...] = jnp.full(m_ref.shape, -jnp.inf, jnp.float32)
    acc_ref[...] = jnp.zeros(acc_ref.shape, jnp.float32)

    def scores(c, h):
        c0 = pl.multiple_of(c * tk, tk)
        s = jnp.dot(k_ref[0, pl.ds(c0, tk), :], qp_ref[h],
                    preferred_element_type=jnp.float32)
        s_ref[h % 2] = s
        cmax_ref[h % 2] = jnp.max(s, axis=0, keepdims=True)

    scores(0, 0)

    def chunk(c, carry):
        c0 = pl.multiple_of(c * tk, tk)
        for h in range(N_HEADS):
            if h + 1 < N_HEADS:
                scores(c, h + 1)
            else:
                scores(jnp.minimum(c + 1, n_chunks - 1), 0)
            m_old = m_ref[h]
            m_new = jnp.maximum(m_old, cmax_ref[h % 2])
            alpha = jnp.exp2(m_old - m_new)
            p = jnp.exp2(s_ref[h % 2] - m_new).astype(jnp.bfloat16)
            vc = v_ref[0, h // GQA_GROUP, :, pl.ds(c0, tk)]
            acc_ref[h] = alpha * acc_ref[h] + jnp.dot(vc, p, preferred_element_type=jnp.float32)
            m_ref[h] = m_new
        return carry

    lax.fori_loop(0, n_chunks, chunk, 0, unroll=2)
    for h in range(N_HEADS):
        a = acc_ref[h]
        o_ref[0, h * HEAD_DIM:(h + 1) * HEAD_DIM, :] = (
            a[0:HEAD_DIM] / a[HEAD_DIM:HEAD_DIM + 1]).astype(jnp.bfloat16)


def _attention(qt, k, vt, tq, tk):
    B, _, S = qt.shape
    return pl.pallas_call(
        functools.partial(_attn_kernel, tk=tk),
        grid=(B, S // tq),
        in_specs=[
            pl.BlockSpec((1, ATTN_WIDTH, tq), lambda b, i: (b, 0, i)),
            pl.BlockSpec((1, S, KV_WIDTH), lambda b, i: (b, 0, 0)),
            pl.BlockSpec((1, N_KV_HEADS, V_ROWS, S), lambda b, i: (b, 0, 0, 0)),
        ],
        out_specs=pl.BlockSpec((1, ATTN_WIDTH, tq), lambda b, i: (b, 0, i)),
        out_shape=jax.ShapeDtypeStruct((B, ATTN_WIDTH, S), jnp.bfloat16),
        scratch_shapes=[
            pltpu.VMEM((N_HEADS, KV_WIDTH, tq), jnp.bfloat16),
            pltpu.VMEM((N_HEADS, 1, tq), jnp.float32),
            pltpu.VMEM((N_HEADS, V_ROWS, tq), jnp.float32),
            pltpu.VMEM((2, tk, tq), jnp.float32),
            pltpu.VMEM((2, 1, tq), jnp.float32),
        ],
        compiler_params=_cp(("parallel", "parallel")),
        name="attention",
    )(qt, k, vt)


def _mix_kernel(x_ref, at_ref, uc_ref, up_ref, un_ref, wp_ref, ps_ref, woa_ref, wop_ref,
                lg_ref, lb_ref, g1_ref, b1_ref, wr_ref, h_ref, hb_ref, aff_ref,
                ue_ref, s2_ref, s4_ref, s8_ref, *, seq_len):
    i = pl.program_id(1)
    T = uc_ref.shape[1]
    n_tiles = seq_len // T
    ue_ref[0:HALO, :] = jnp.where(i > 0, up_ref[0], 0.0)
    ue_ref[HALO:HALO + T, :] = uc_ref[0]
    ue_ref[HALO + T:HALO + T + HALO, :] = jnp.where(i < n_tiles - 1, un_ref[0], 0.0)
    ue_ref[T + 2 * HALO:, :] = jnp.zeros((8, POOL_WIDTH), jnp.float32)
    wide = slice(2 * POOL_GROUP_W, POOL_WIDTH)
    s2_ref[8:T + 32, :] = ue_ref[8:T + 32, wide] + ue_ref[9:T + 33, wide]
    s4_ref[8:T + 24, :] = s2_ref[8:T + 24, :] + s2_ref[10:T + 26, :]
    s8_ref[8:T + 16, :] = (s4_ref[8:T + 16, POOL_GROUP_W:] + s4_ref[12:T + 20, POOL_GROUP_W:])
    wins = [
        ue_ref[HALO - 1:HALO - 1 + T, 0:POOL_GROUP_W] + ue_ref[HALO:HALO + T, 0:POOL_GROUP_W],
        (ue_ref[HALO - 2:HALO - 2 + T, POOL_GROUP_W:2 * POOL_GROUP_W]
         + ue_ref[HALO - 1:HALO - 1 + T, POOL_GROUP_W:2 * POOL_GROUP_W]
         + ue_ref[HALO:HALO + T, POOL_GROUP_W:2 * POOL_GROUP_W]
         + ue_ref[HALO + 1:HALO + 1 + T, POOL_GROUP_W:2 * POOL_GROUP_W]),
        s4_ref[HALO - 4:HALO - 4 + T, 0:POOL_GROUP_W] + s4_ref[HALO:HALO + T, 0:POOL_GROUP_W],
        s8_ref[HALO - 8:HALO - 8 + T, :] + s8_ref[HALO:HALO + T, :],
    ]
    tg = (i * T + lax.broadcasted_iota(jnp.int32, (T, 1), 0))
    pooled = []
    for g, w in enumerate(POOL_WINDOWS):
        cols = slice(g * POOL_GROUP_W, (g + 1) * POOL_GROUP_W)
        win = wins[g]
        lo = jnp.maximum(tg - w // 2, 0)
        hi = jnp.minimum(tg - w // 2 + w, seq_len)
        cnt = (hi - lo).astype(jnp.float32)
        m = win / cnt - uc_ref[0, :, cols]
        y = jnp.dot(m.astype(jnp.bfloat16), wp_ref[g], preferred_element_type=jnp.float32)
        pooled.append(y * ps_ref[:, cols])
    pool = jnp.concatenate(pooled, axis=1).astype(jnp.bfloat16)
    mix = lax.dot_general(at_ref[0], woa_ref[...], (((0,), (0,)), ((), ())),
                          preferred_element_type=jnp.float32)
    mix = mix + jnp.dot(pool, wop_ref[...], preferred_element_type=jnp.float32)
    x0 = _ln(x_ref[0], lg_ref[...], lb_ref[...])
    h = _ln(ALPHA * x0 + mix, g1_ref[...], b1_ref[...])
    h_ref[0] = h
    hb = h.astype(jnp.bfloat16)
    hb_ref[0] = hb
    logits = lax.dot_general(wr_ref[...], hb, (((1,), (1,)), ((), ())),
                             preferred_element_type=jnp.float32)
    e = jnp.exp(logits - jnp.max(logits, axis=0, keepdims=True))
    aff_ref[...] = e / jnp.sum(e, axis=0, keepdims=True)


def _mix_out(x, attn_t, u, wp, ps, woa, wop, lg, lb, g1, b1, wr_t, tile):
    B, S, _ = x.shape
    nt = S // tile
    hb = tile // HALO
    const = lambda shape: pl.BlockSpec(shape, lambda b, i: tuple(0 for _ in shape))
    return pl.pallas_call(
        functools.partial(_mix_kernel, seq_len=S),
        grid=(B, nt),
        in_specs=[
            pl.BlockSpec((1, tile, D_MODEL), lambda b, i: (b, i, 0)),
            pl.BlockSpec((1, ATTN_WIDTH, tile), lambda b, i: (b, 0, i)),
            pl.BlockSpec((1, tile, POOL_WIDTH), lambda b, i: (b, i, 0)),
            pl.BlockSpec((1, HALO, POOL_WIDTH), lambda b, i: (b, jnp.maximum(i * hb - 1, 0), 0)),
            pl.BlockSpec((1, HALO, POOL_WIDTH),
                         lambda b, i: (b, jnp.minimum((i + 1) * hb, S // HALO - 1), 0)),
            const(wp.shape), const((1, POOL_WIDTH)),
            const((ATTN_WIDTH, D_MODEL)), const((POOL_WIDTH, D_MODEL)),
            const((1, D_MODEL)), const((1, D_MODEL)), const((1, D_MODEL)), const((1, D_MODEL)),
            const((N_EXPERTS, D_MODEL)),
        ],
        out_specs=[
            pl.BlockSpec((1, tile, D_MODEL), lambda b, i: (b, i, 0)),
            pl.BlockSpec((1, tile, D_MODEL), lambda b, i: (b, i, 0)),
            pl.BlockSpec((N_EXPERTS, tile), lambda b, i: (0, b * nt + i)),
        ],
        out_shape=[
            jax.ShapeDtypeStruct((B, S, D_MODEL), jnp.float32),
            jax.ShapeDtypeStruct((B, S, D_MODEL), jnp.bfloat16),
            jax.ShapeDtypeStruct((N_EXPERTS, B * S), jnp.float32),
        ],
        scratch_shapes=[
            pltpu.VMEM((tile + 2 * HALO + 8, POOL_WIDTH), jnp.float32),
            pltpu.VMEM((tile + 2 * HALO + 8, 2 * POOL_GROUP_W), jnp.float32),
            pltpu.VMEM((tile + 2 * HALO + 8, 2 * POOL_GROUP_W), jnp.float32),
            pltpu.VMEM((tile + 2 * HALO + 8, POOL_GROUP_W), jnp.float32),
        ],
        compiler_params=_cp(("parallel", "parallel")),
        name="mix_out",
    )(x, attn_t, u, u, u, wp, ps, woa, wop, lg, lb, g1, b1, wr_t)


def _route_kernel(aff_ref, tri_ref, ones_ref, low_ref, pos_ref, roff_ref, thr_ref, *, cap):
    n_e, R, _ = aff_ref.shape
    tri = tri_ref[...]
    ones = ones_ref[...]
    low = low_ref[...]

    def excl_prefix(flag):
        fb = flag.astype(jnp.bfloat16)
        incl = jnp.dot(fb, tri, preferred_element_type=jnp.float32)
        tot = jnp.dot(fb, ones, preferred_element_type=jnp.float32)
        before = jnp.dot(low, tot.astype(jnp.bfloat16), preferred_element_type=jnp.float32)
        return before + incl - flag, before

    for e in range(n_e):
        thr_ref[e] = 0

    def at_least(aff, bits):
        as_float = jnp.where(aff >= lax.bitcast_convert_type(bits, jnp.float32), 1.0, 0.0)
        as_bits = jnp.where(pltpu.bitcast(aff, jnp.int32) >= bits, 1.0, 0.0)
        return jnp.where(bits < F32_MIN_NORMAL_BITS, as_bits, as_float)

    def bit_step(it, carry):
        bit = 30 - it
        for e in range(n_e):
            cand = thr_ref[e] | lax.shift_left(jnp.int32(1), bit)
            cnt = jnp.sum(at_least(aff_ref[e], cand))
            thr_ref[e] = jnp.where(cnt >= float(cap), cand, thr_ref[e])
        return carry

    lax.fori_loop(0, 31, bit_step, 0)

    for e in range(n_e):
        aff = aff_ref[e]
        thr = thr_ref[e]
        gt = at_least(aff, thr + 1)
        eq = at_least(aff, thr) - gt
        need = float(cap) - jnp.sum(gt)
        eq_rank, _ = excl_prefix(eq)
        sel = gt + eq * jnp.where(eq_rank < need, 1.0, 0.0)
        pos, before = excl_prefix(sel)
        pos_ref[e] = jnp.where(sel > 0.5, pos, -1.0).astype(jnp.int32)
        roff_ref[e] = before.astype(jnp.int32)


def _route(aff3, cap):
    n_e, R, _ = aff3.shape
    tri = jnp.asarray(np.triu(np.ones((LANES, LANES), np.float32)), jnp.bfloat16)
    ones = jnp.ones((LANES, LANES), jnp.bfloat16)
    low = jnp.asarray(np.tril(np.ones((R, R), np.float32), -1), jnp.bfloat16)
    return pl.pallas_call(
        functools.partial(_route_kernel, cap=cap),
        out_shape=[jax.ShapeDtypeStruct((n_e, R, LANES), jnp.int32),
                   jax.ShapeDtypeStruct((n_e, R, LANES), jnp.int32)],
        scratch_shapes=[pltpu.SMEM((n_e,), jnp.int32)],
        compiler_params=pltpu.CompilerParams(vmem_limit_bytes=V7X_VMEM_LIMIT),
        name="route",
    )(aff3, tri, ones, low)


def _slot_base(off_ref, e, j, p, nt1):
    a = off_ref[e * nt1 + j]
    a16 = lax.shift_left(lax.shift_right_logical(a, 4), 4)
    return a16 + p * SLOT_WIN


def _window_onehot(pos_ref, rel_bases, min_pos, tile, value_rows=None):
    rows = lax.broadcasted_iota(jnp.int32, (SLOT_WIN, tile), 0)
    blocks = []
    for e in range(N_EXPERTS):
        pos = pos_ref[e:e + 1, :]
        rel = jnp.where(pos >= min_pos[e], pos - rel_bases[e], -1)
        hit = rows == rel
        if value_rows is None:
            blocks.append(jnp.where(hit, 1.0, 0.0).astype(jnp.bfloat16))
        else:
            blocks.append(jnp.where(hit, value_rows[e:e + 1, :], 0.0).astype(jnp.bfloat16))
    return jnp.concatenate(blocks, axis=0)


def _gather_kernel(off_ref, np_ref, pos_ref, h_ref, xe_ref, stage_ref, carry_ref, par_ref, sem,
                   *, nt, cap):
    j = pl.program_id(0)
    nt1 = nt + 1
    tile = h_ref.shape[0]

    def flush_copy(par, e, base):
        return pltpu.make_async_copy(stage_ref.at[par, e], xe_ref.at[e, pl.ds(base, SLOT_WIN)],
                                     sem.at[e])

    @pl.when(j == 0)
    def _():
        carry_ref[...] = jnp.zeros(carry_ref.shape, jnp.bfloat16)
        par_ref[0] = 0
        stage_ref[1] = jnp.zeros(stage_ref.shape[1:], jnp.bfloat16)
        for e in range(N_EXPERTS):
            flush_copy(1, e, cap).start()

    hb = h_ref[...]

    def one_pass(p, carry):
        par = par_ref[0]
        bases = [_slot_base(off_ref, e, j, p, nt1) for e in range(N_EXPERTS)]
        onehot = _window_onehot(pos_ref, bases, bases, tile)
        rows = jnp.dot(onehot, hb, preferred_element_type=jnp.float32).astype(jnp.bfloat16)
        stage_ref[par] = rows.reshape(N_EXPERTS, SLOT_WIN, D_MODEL)
        for e in range(N_EXPERTS):
            flush_copy(1 - par, e, 0).wait()
        for e in range(N_EXPERTS):
            b = off_ref[e * nt1 + j + 1]
            active = bases[e] <= b
            carry_in = jnp.where(p == 0, carry_ref[e], jnp.zeros_like(carry_ref[e]))
            stage_ref[par, e, 0:BF16_SUBLANES, :] = stage_ref[par, e, 0:BF16_SUBLANES, :] + carry_in
            dst = jnp.where(active, bases[e], cap)
            flush_copy(par, e, pl.multiple_of(dst, BF16_SUBLANES)).start()
            last = jnp.logical_and(active, b < bases[e] + SLOT_WIN)
            g = lax.shift_left(lax.shift_right_logical(b, 4), 4) - bases[e]
            g = jnp.clip(g, 0, SLOT_WIN - BF16_SUBLANES)
            tail = stage_ref[par, e, pl.ds(pl.multiple_of(g, BF16_SUBLANES), BF16_SUBLANES), :]
            carry_ref[e] = jnp.where(last, tail, carry_ref[e])
        par_ref[0] = 1 - par
        return carry

    lax.fori_loop(0, np_ref[j], one_pass, 0)

    @pl.when(j == nt - 1)
    def _():
        par = par_ref[0]
        for e in range(N_EXPERTS):
            flush_copy(1 - par, e, 0).wait()
        stage_ref[par] = jnp.zeros(stage_ref.shape[1:], jnp.bfloat16)
        for e in range(N_EXPERTS):
            flush_copy(par, e, cap).start()
        for e in range(N_EXPERTS):
            flush_copy(par, e, cap).wait()


def _gather(off, npass, pos, h, cap):
    N = h.shape[0]
    nt = N // TOK_TILE
    return pl.pallas_call(
        functools.partial(_gather_kernel, nt=nt, cap=cap),
        grid_spec=pltpu.PrefetchScalarGridSpec(
            num_scalar_prefetch=2,
            grid=(nt,),
            in_specs=[
                pl.BlockSpec((N_EXPERTS, TOK_TILE), lambda j, *_: (0, j)),
                pl.BlockSpec((TOK_TILE, D_MODEL), lambda j, *_: (j, 0)),
            ],
            out_specs=pl.BlockSpec(memory_space=pl.ANY),
            scratch_shapes=[
                pltpu.VMEM((2, N_EXPERTS, SLOT_WIN, D_MODEL), jnp.bfloat16),
                pltpu.VMEM((N_EXPERTS, BF16_SUBLANES, D_MODEL), jnp.bfloat16),
                pltpu.SMEM((1,), jnp.int32),
                pltpu.SemaphoreType.DMA((N_EXPERTS,)),
            ],
        ),
        out_shape=jax.ShapeDtypeStruct((N_EXPERTS, cap + SLOT_WIN, D_MODEL), jnp.bfloat16),
        compiler_params=_cp(("arbitrary",)),
        name="gather",
    )(off, npass, pos, h)


def _ffn_kernel(x_ref, wg_ref, wu_ref, wd_ref, y_ref):
    x = x_ref[0]
    half = D_FF // 2
    acc = None
    for c in range(2):
        cols = slice(c * half, (c + 1) * half)
        g = jnp.dot(x, wg_ref[0, :, cols], preferred_element_type=jnp.float32)
        u = jnp.dot(x, wu_ref[0, :, cols], preferred_element_type=jnp.float32)
        hid = (g * jax.nn.sigmoid(g) * u).astype(jnp.bfloat16)
        part = jnp.dot(hid, wd_ref[0, cols, :], preferred_element_type=jnp.float32)
        acc = part if acc is None else acc + part
    y_ref[0] = acc.astype(jnp.bfloat16)


def _ffn(xe, wg, wu, wd, cap, tile):
    n_e = xe.shape[0]
    return pl.pallas_call(
        _ffn_kernel,
        grid=(n_e, cap // tile),
        in_specs=[
            pl.BlockSpec((1, tile, D_MODEL), lambda e, s: (e, s, 0)),
            pl.BlockSpec((1, D_MODEL, D_FF), lambda e, s: (e, 0, 0)),
            pl.BlockSpec((1, D_MODEL, D_FF), lambda e, s: (e, 0, 0)),
            pl.BlockSpec((1, D_FF, D_MODEL), lambda e, s: (e, 0, 0)),
        ],
        out_specs=pl.BlockSpec((1, tile, D_MODEL), lambda e, s: (e, s, 0)),
        out_shape=jax.ShapeDtypeStruct((n_e, cap, D_MODEL), jnp.bfloat16),
        compiler_params=_cp(("parallel", "arbitrary")),
        name="ffn",
    )(xe, wg, wu, wd)


def _combine_kernel(off_ref, np_ref, pos_ref, aff_ref, h_ref, g2_ref, b2_ref, ye_ref, y_ref,
                    win_ref, slot_ref, sem, *, nt, cap):
    j = pl.program_id(0)
    nt1 = nt + 1
    tile = h_ref.shape[0]

    def fetch_copy(e, base, slot):
        return pltpu.make_async_copy(ye_ref.at[e, pl.ds(base, SLOT_WIN)], win_ref.at[slot, e],
                                     sem.at[slot])

    def read_base(e, jj, p):
        return jnp.minimum(_slot_base(off_ref, e, jj, p, nt1), cap - SLOT_WIN)

    def start_fetch(jj, p, slot):
        for e in range(N_EXPERTS):
            fetch_copy(e, pl.multiple_of(read_base(e, jj, p), BF16_SUBLANES), slot).start()

    def wait_fetch(slot):
        for e in range(N_EXPERTS):
            fetch_copy(e, 0, slot).wait()

    @pl.when(j == 0)
    def _():
        slot_ref[0] = 0
        start_fetch(0, 0, 0)

    n_pass = np_ref[j]
    gates = aff_ref[...]

    def one_pass(p):
        slot = slot_ref[0]
        wait_fetch(slot)

        @pl.when(p + 1 < n_pass)
        def _():
            start_fetch(j, p + 1, 1 - slot)

        @pl.when(jnp.logical_and(p + 1 >= n_pass, j + 1 < nt))
        def _():
            start_fetch(j + 1, 0, 1 - slot)

        onehot = _window_onehot(pos_ref,
                                [read_base(e, j, p) for e in range(N_EXPERTS)],
                                [_slot_base(off_ref, e, j, p, nt1) for e in range(N_EXPERTS)],
                                tile, value_rows=gates)
        rows = win_ref[slot].reshape(N_EXPERTS * SLOT_WIN, D_MODEL)
        slot_ref[0] = 1 - slot
        return lax.dot_general(onehot, rows, (((0,), (0,)), ((), ())),
                               preferred_element_type=jnp.float32)

    ff = lax.fori_loop(1, n_pass, lambda p, ff: ff + one_pass(p), one_pass(0))
    y_ref[...] = _ln(ALPHA * h_ref[...] + ff, g2_ref[...], b2_ref[...])


def _combine(off, npass, pos, aff, h, g2, b2, ye, cap):
    N = h.shape[0]
    nt = N // TOK_TILE
    return pl.pallas_call(
        functools.partial(_combine_kernel, nt=nt, cap=cap),
        grid_spec=pltpu.PrefetchScalarGridSpec(
            num_scalar_prefetch=2,
            grid=(nt,),
            in_specs=[
                pl.BlockSpec((N_EXPERTS, TOK_TILE), lambda j, *_: (0, j)),
                pl.BlockSpec((N_EXPERTS, TOK_TILE), lambda j, *_: (0, j)),
                pl.BlockSpec((TOK_TILE, D_MODEL), lambda j, *_: (j, 0)),
                pl.BlockSpec((1, D_MODEL), lambda j, *_: (0, 0)),
                pl.BlockSpec((1, D_MODEL), lambda j, *_: (0, 0)),
                pl.BlockSpec(memory_space=pl.ANY),
            ],
            out_specs=pl.BlockSpec((TOK_TILE, D_MODEL), lambda j, *_: (j, 0)),
            scratch_shapes=[
                pltpu.VMEM((2, N_EXPERTS, SLOT_WIN, D_MODEL), jnp.bfloat16),
                pltpu.SMEM((1,), jnp.int32),
                pltpu.SemaphoreType.DMA((2,)),
            ],
        ),
        out_shape=jax.ShapeDtypeStruct((N, D_MODEL), jnp.float32),
        compiler_params=_cp(("arbitrary",)),
        name="combine",
    )(off, npass, pos, aff, h, g2, b2, ye)


def _rope_tables(seq_len):
    rows = seq_len // GRID_W
    row = jnp.repeat(jnp.arange(rows), GRID_W).astype(jnp.float32)
    col = jnp.tile(jnp.arange(GRID_W), rows).astype(jnp.float32)
    inv_freq = ROPE_THETA ** (-jnp.arange(ROT_PAIRS, dtype=jnp.float32) / ROT_PAIRS)
    ang_r = inv_freq[:, None] * row[None, :]
    ang_c = inv_freq[:, None] * col[None, :]
    ct = jnp.concatenate([jnp.cos(ang_r), jnp.cos(ang_r), jnp.cos(ang_c), jnp.cos(ang_c)], axis=0)
    st = jnp.concatenate([-jnp.sin(ang_r), jnp.sin(ang_r), -jnp.sin(ang_c), jnp.sin(ang_c)], axis=0)
    return ct, st


def _trunk(x, prm, *, proj_tile, attn_tq, attn_tk, mix_tile, ffn_tile):
    B, S, _ = x.shape
    N = B * S
    cap = (CAPACITY_FACTOR * N) // N_EXPERTS
    assert S % proj_tile == 0 and S % attn_tq == 0 and S % attn_tk == 0 and S % mix_tile == 0
    assert N % TOK_TILE == 0 and TOK_TILE % LANES == 0
    assert cap % BF16_SUBLANES == 0 and cap >= SLOT_WIN and cap % ffn_tile == 0

    ct, st = _rope_tables(S)
    qt, k, vt, u = _in_proj(x, prm["ln_in_g"], prm["ln_in_b"], prm["wt"], prm["wu"], ct, st,
                            prm["gq"], prm["gk"], proj_tile)
    attn_t = _attention(qt, k, vt, attn_tq, attn_tk)
    h, hb, aff = _mix_out(x, attn_t, u, prm["w_pool"], prm["pool_scale"],
                      prm["wo_attn"], prm["wo_pool"], prm["ln_in_g"], prm["ln_in_b"],
                      prm["ln1_g"], prm["ln1_b"], prm["wr_t"], mix_tile)
    h = h.reshape(N, D_MODEL)

    pos3, roff3 = _route(aff.reshape(N_EXPERTS, N // LANES, LANES), cap)
    pos = pos3.reshape(N_EXPERTS, N)
    off = jnp.concatenate([roff3[:, ::TOK_TILE // LANES, 0],
                           jnp.full((N_EXPERTS, 1), cap, jnp.int32)], axis=1)
    a, b = off[:, :-1], off[:, 1:]
    a16 = (a // BF16_SUBLANES) * BF16_SUBLANES
    npass = jnp.max((b - a16) // SLOT_WIN + 1, axis=0).astype(jnp.int32)
    off = off.reshape(-1)

    xe = _gather(off, npass, pos, hb.reshape(N, D_MODEL), cap)
    ye = _ffn(xe, prm["w_gate"], prm["w_up"], prm["w_down"], cap, ffn_tile)
    y = _combine(off, npass, pos, aff, h, prm["ln2_g"], prm["ln2_b"], ye, cap)
    return y.reshape(B, S, D_MODEL)


def _prepare(ln_in_g, ln_in_b, w_in, q_norm_g, k_norm_g, w_pool, pool_scale, w_out, ln1_g, ln1_b,
             w_router, w_gate, w_up, w_down, ln2_g, ln2_b):
    bf = jnp.bfloat16
    row = lambda v: v.reshape(1, -1).astype(jnp.float32)
    qkv = ATTN_WIDTH + 2 * KV_WIDTH
    return {
        "ln_in_g": row(ln_in_g), "ln_in_b": row(ln_in_b),
        "wt": jnp.transpose(w_in[0][:, :qkv]).astype(bf),
        "wu": w_in[0][:, qkv:].astype(bf),
        "gq": q_norm_g[0].reshape(HEAD_DIM, 1), "gk": k_norm_g[0].reshape(HEAD_DIM, 1),
        "w_pool": w_pool[0].astype(bf), "pool_scale": row(pool_scale[0]),
        "wo_attn": w_out[0][:ATTN_WIDTH].astype(bf), "wo_pool": w_out[0][ATTN_WIDTH:].astype(bf),
        "ln1_g": row(ln1_g[0]), "ln1_b": row(ln1_b[0]),
        "wr_t": jnp.transpose(w_router[0]).astype(bf),
        "w_gate": w_gate[0].astype(bf), "w_up": w_up[0].astype(bf), "w_down": w_down[0].astype(bf),
        "ln2_g": row(ln2_g[0]), "ln2_b": row(ln2_b[0]),
    }


def kernel(x_prompt, x_sample, ln_in_g, ln_in_b, w_in, q_norm_g, k_norm_g, w_pool, pool_scale, w_out,
           ln1_g, ln1_b, w_router, w_gate, w_up, w_down, ln2_g, ln2_b):
    prm = _prepare(ln_in_g, ln_in_b, w_in, q_norm_g, k_norm_g, w_pool, pool_scale, w_out, ln1_g, ln1_b,
                   w_router, w_gate, w_up, w_down, ln2_g, ln2_b)
    tiles = dict(proj_tile=1024, attn_tq=512, attn_tk=1024, mix_tile=512, ffn_tile=1024)
    return (_trunk(x_prompt, prm, **tiles), _trunk(x_sample, prm, **tiles))
```

```python
import functools
import math

import jax
import jax.numpy as jnp
import numpy as np
from jax import lax
from jax.experimental import pallas as pl
from jax.experimental.pallas import tpu as pltpu

D_MODEL = 1024
ATTN_WIDTH = 512
POOL_WIDTH = 512
HEAD_DIM = 64
N_HEADS = 8
N_KV_HEADS = 2
GQA_GROUP = N_HEADS // N_KV_HEADS
KV_WIDTH = N_KV_HEADS * HEAD_DIM
ROT_PAIRS = HEAD_DIM // 4
ROPE_THETA = 10000.0
GRID_W = 64
POOL_WINDOWS = (2, 4, 8, 16)
POOL_GROUP_W = 128
N_EXPERTS = 16
CAPACITY_FACTOR = 2
D_FF = 2048
LN_EPS = 1e-5
RMS_EPS = 1e-6
ALPHA = 2.0 ** 0.25

LANES = 128
BF16_SUBLANES = 16
F32_MIN_NORMAL_BITS = 0x00800000
V7X_VMEM_LIMIT = 56 * 1024 * 1024

HALO = 16
V_ROWS = 80
SLOT_WIN = 64
TOK_TILE = 256


def _ln(x, g, b):
    mu = jnp.mean(x, axis=-1, keepdims=True)
    xc = x - mu
    var = jnp.mean(xc * xc, axis=-1, keepdims=True)
    return xc * lax.rsqrt(var + LN_EPS) * g + b


def _cp(sem, vmem=V7X_VMEM_LIMIT, flags=None):
    return pltpu.CompilerParams(dimension_semantics=sem, vmem_limit_bytes=vmem, flags=flags)


def _in_proj_kernel(x_ref, g_ref, b_ref, wt_ref, wu_ref, ct_ref, st_ref, gq_ref, gk_ref,
                    q_ref, k_ref, v_ref, u_ref, *, q_scale):
    x0 = _ln(x_ref[0], g_ref[...], b_ref[...])
    xb = x0.astype(jnp.bfloat16)
    pt = lax.dot_general(wt_ref[...], xb, (((1,), (1,)), ((), ())),
                         preferred_element_type=jnp.float32)
    u_ref[0] = jnp.dot(xb, wu_ref[...], preferred_element_type=jnp.float32)
    ct = ct_ref[...]
    st = st_ref[...]

    def norm_rope(xh, gcol):
        ms = jnp.mean(xh * xh, axis=0, keepdims=True)
        xn = xh * lax.rsqrt(ms + RMS_EPS) * gcol
        partner = jnp.concatenate([xn[16:32], xn[0:16], xn[48:64], xn[32:48]], axis=0)
        return xn * ct + partner * st

    gq = gq_ref[...]
    gk = gk_ref[...]
    for h in range(N_HEADS):
        qh = norm_rope(pt[h * HEAD_DIM:(h + 1) * HEAD_DIM], gq) * q_scale
        q_ref[0, h * HEAD_DIM:(h + 1) * HEAD_DIM, :] = qh.astype(jnp.bfloat16)
    kt = jnp.concatenate(
        [norm_rope(pt[ATTN_WIDTH + h * HEAD_DIM:ATTN_WIDTH + (h + 1) * HEAD_DIM], gk)
         for h in range(N_KV_HEADS)], axis=0)
    k_ref[0] = jnp.transpose(kt).astype(jnp.bfloat16)
    ones = jnp.ones((V_ROWS - HEAD_DIM, pt.shape[1]), jnp.bfloat16)
    for h in range(N_KV_HEADS):
        r0 = ATTN_WIDTH + KV_WIDTH + h * HEAD_DIM
        v_ref[0, h, 0:HEAD_DIM, :] = pt[r0:r0 + HEAD_DIM].astype(jnp.bfloat16)
        v_ref[0, h, HEAD_DIM:V_ROWS, :] = ones


def _in_proj(x, ln_g, ln_b, wt, wu, ct, st, gq, gk, tile):
    B, S, _ = x.shape
    q_scale = (HEAD_DIM ** -0.5) * math.log2(math.e)
    const = lambda shape: pl.BlockSpec(shape, lambda b, i: tuple(0 for _ in shape))
    return pl.pallas_call(
        functools.partial(_in_proj_kernel, q_scale=q_scale),
        grid=(B, S // tile),
        in_specs=[
            pl.BlockSpec((1, tile, D_MODEL), lambda b, i: (b, i, 0)),
            const((1, D_MODEL)), const((1, D_MODEL)),
            const((ATTN_WIDTH + 2 * KV_WIDTH, D_MODEL)),
            const((D_MODEL, POOL_WIDTH)),
            pl.BlockSpec((HEAD_DIM, tile), lambda b, i: (0, i)),
            pl.BlockSpec((HEAD_DIM, tile), lambda b, i: (0, i)),
            const((HEAD_DIM, 1)), const((HEAD_DIM, 1)),
        ],
        out_specs=[
            pl.BlockSpec((1, ATTN_WIDTH, tile), lambda b, i: (b, 0, i)),
            pl.BlockSpec((1, tile, KV_WIDTH), lambda b, i: (b, i, 0)),
            pl.BlockSpec((1, N_KV_HEADS, V_ROWS, tile), lambda b, i: (b, 0, 0, i)),
            pl.BlockSpec((1, tile, POOL_WIDTH), lambda b, i: (b, i, 0)),
        ],
        out_shape=[
            jax.ShapeDtypeStruct((B, ATTN_WIDTH, S), jnp.bfloat16),
            jax.ShapeDtypeStruct((B, S, KV_WIDTH), jnp.bfloat16),
            jax.ShapeDtypeStruct((B, N_KV_HEADS, V_ROWS, S), jnp.bfloat16),
            jax.ShapeDtypeStruct((B, S, POOL_WIDTH), jnp.float32),
        ],
        compiler_params=_cp(("parallel", "parallel")),
        name="in_proj",
    )(x, ln_g, ln_b, wt, wu, ct, st, gq, gk)


def _attn_kernel(q_ref, k_ref, v_ref, o_ref, qp_ref, m_ref, acc_ref, s_ref, cmax_ref, *, tk):
    S = k_ref.shape[1]
    tq = q_ref.shape[2]
    n_chunks = S // tk
    zeros = jnp.zeros((HEAD_DIM, GQA_GROUP * tq), jnp.bfloat16)
    for g in range(N_KV_HEADS):
        qg = jnp.concatenate(
            [q_ref[0, (g * GQA_GROUP + j) * HEAD_DIM:(g * GQA_GROUP + j + 1) * HEAD_DIM, :]
             for j in range(GQA_GROUP)], axis=1)
        qp_ref[g] = jnp.concatenate([qg, zeros] if g == 0 else [zeros, qg], axis=0)
    m_ref[...] = jnp.full(m_ref.shape, -jnp.inf, jnp.float32)
    acc_ref[...] = jnp.zeros(acc_ref.shape, jnp.float32)

    def scores(c, g):
        c0 = pl.multiple_of(c * tk, tk)
        s = jnp.dot(k_ref[0, pl.ds(c0, tk), :], qp_ref[g],
                    preferred_element_type=jnp.float32)
        s_ref[g] = s
        cmax_ref[g] = jnp.max(s, axis=0, keepdims=True)

    scores(0, 0)

    def chunk(c, carry):
        c0 = pl.multiple_of(c * tk, tk)
        for g in range(N_KV_HEADS):
            if g + 1 < N_KV_HEADS:
                scores(c, g + 1)
            else:
                scores(jnp.minimum(c + 1, n_chunks - 1), 0)
            m_old = m_ref[g]
            m_new = jnp.maximum(m_old, cmax_ref[g])
            alpha = jnp.exp2(m_old - m_new)
            p = jnp.exp2(s_ref[g] - m_new).astype(jnp.bfloat16)
            vc = v_ref[0, g, :, pl.ds(c0, tk)]
            acc_ref[g] = alpha * acc_ref[g] + jnp.dot(vc, p, preferred_element_type=jnp.float32)
            m_ref[g] = m_new
        return carry

    lax.fori_loop(0, n_chunks, chunk, 0, unroll=2)
    for h in range(N_HEADS):
        g, j = divmod(h, GQA_GROUP)
        a = acc_ref[g, :, j * tq:(j + 1) * tq]
        o_ref[0, h * HEAD_DIM:(h + 1) * HEAD_DIM, :] = (
            a[0:HEAD_DIM] / a[HEAD_DIM:HEAD_DIM + 1]).astype(jnp.bfloat16)


def _attention(qt, k, vt, tq, tk):
    B, _, S = qt.shape
    return pl.pallas_call(
        functools.partial(_attn_kernel, tk=tk),
        grid=(B, S // tq),
        in_specs=[
            pl.BlockSpec((1, ATTN_WIDTH, tq), lambda b, i: (b, 0, i)),
            pl.BlockSpec((1, S, KV_WIDTH), lambda b, i: (b, 0, 0)),
            pl.BlockSpec((1, N_KV_HEADS, V_ROWS, S), lambda b, i: (b, 0, 0, 0)),
        ],
        out_specs=pl.BlockSpec((1, ATTN_WIDTH, tq), lambda b, i: (b, 0, i)),
        out_shape=jax.ShapeDtypeStruct((B, ATTN_WIDTH, S), jnp.bfloat16),
        scratch_shapes=[
            pltpu.VMEM((N_KV_HEADS, KV_WIDTH, GQA_GROUP * tq), jnp.bfloat16),
            pltpu.VMEM((N_KV_HEADS, 1, GQA_GROUP * tq), jnp.float32),
            pltpu.VMEM((N_KV_HEADS, V_ROWS, GQA_GROUP * tq), jnp.float32),
            pltpu.VMEM((N_KV_HEADS, tk, GQA_GROUP * tq), jnp.float32),
            pltpu.VMEM((N_KV_HEADS, 1, GQA_GROUP * tq), jnp.float32),
        ],
        compiler_params=_cp(("parallel", "parallel")),
        name="attention",
    )(qt, k, vt)


def _mix_kernel(x_ref, at_ref, uc_ref, up_ref, un_ref, wp_ref, ps_ref, woa_ref, wop_ref,
                lg_ref, lb_ref, g1_ref, b1_ref, wr_ref, h_ref, hb_ref, aff_ref,
                ue_ref, s2_ref, s4_ref, s8_ref, *, seq_len):
    i = pl.program_id(1)
    T = uc_ref.shape[1]
    n_tiles = seq_len // T
    ue_ref[0:HALO, :] = jnp.where(i > 0, up_ref[0], 0.0)
    ue_ref[HALO:HALO + T, :] = uc_ref[0]
    ue_ref[HALO + T:HALO + T + HALO, :] = jnp.where(i < n_tiles - 1, un_ref[0], 0.0)
    ue_ref[T + 2 * HALO:, :] = jnp.zeros((8, POOL_WIDTH), jnp.float32)
    wide = slice(2 * POOL_GROUP_W, POOL_WIDTH)
    s2_ref[8:T + 32, :] = ue_ref[8:T + 32, wide] + ue_ref[9:T + 33, wide]
    s4_ref[8:T + 24, :] = s2_ref[8:T + 24, :] + s2_ref[10:T + 26, :]
    s8_ref[8:T + 16, :] = (s4_ref[8:T + 16, POOL_GROUP_W:] + s4_ref[12:T + 20, POOL_GROUP_W:])
    wins = [
        ue_ref[HALO - 1:HALO - 1 + T, 0:POOL_GROUP_W] + ue_ref[HALO:HALO + T, 0:POOL_GROUP_W],
        (ue_ref[HALO - 2:HALO - 2 + T, POOL_GROUP_W:2 * POOL_GROUP_W]
         + ue_ref[HALO - 1:HALO - 1 + T, POOL_GROUP_W:2 * POOL_GROUP_W]
         + ue_ref[HALO:HALO + T, POOL_GROUP_W:2 * POOL_GROUP_W]
         + ue_ref[HALO + 1:HALO + 1 + T, POOL_GROUP_W:2 * POOL_GROUP_W]),
        s4_ref[HALO - 4:HALO - 4 + T, 0:POOL_GROUP_W] + s4_ref[HALO:HALO + T, 0:POOL_GROUP_W],
        s8_ref[HALO - 8:HALO - 8 + T, :] + s8_ref[HALO:HALO + T, :],
    ]
    tg = (i * T + lax.broadcasted_iota(jnp.int32, (T, 1), 0))
    pooled = []
    for g, w in enumerate(POOL_WINDOWS):
        cols = slice(g * POOL_GROUP_W, (g + 1) * POOL_GROUP_W)
        win = wins[g]
        lo = jnp.maximum(tg - w // 2, 0)
        hi = jnp.minimum(tg - w // 2 + w, seq_len)
        cnt = (hi - lo).astype(jnp.float32)
        m = win / cnt - uc_ref[0, :, cols]
        y = jnp.dot(m.astype(jnp.bfloat16), wp_ref[g], preferred_element_type=jnp.float32)
        pooled.append(y * ps_ref[:, cols])
    pool = jnp.concatenate(pooled, axis=1).astype(jnp.bfloat16)
    mix = lax.dot_general(at_ref[0], woa_ref[...], (((0,), (0,)), ((), ())),
                          preferred_element_type=jnp.float32)
    mix = mix + jnp.dot(pool, wop_ref[...], preferred_element_type=jnp.float32)
    x0 = _ln(x_ref[0], lg_ref[...], lb_ref[...])
    h = _ln(ALPHA * x0 + mix, g1_ref[...], b1_ref[...])
    h_ref[0] = h
    hb = h.astype(jnp.bfloat16)
    hb_ref[0] = hb
    logits = lax.dot_general(wr_ref[...], hb, (((1,), (1,)), ((), ())),
                             preferred_element_type=jnp.float32)
    e = jnp.exp(logits - jnp.max(logits, axis=0, keepdims=True))
    aff_ref[...] = e / jnp.sum(e, axis=0, keepdims=True)


def _mix_out(x, attn_t, u, wp, ps, woa, wop, lg, lb, g1, b1, wr_t, tile):
    B, S, _ = x.shape
    nt = S // tile
    hb = tile // HALO
    const = lambda shape: pl.BlockSpec(shape, lambda b, i: tuple(0 for _ in shape))
    return pl.pallas_call(
        functools.partial(_mix_kernel, seq_len=S),
        grid=(B, nt),
        in_specs=[
            pl.BlockSpec((1, tile, D_MODEL), lambda b, i: (b, i, 0)),
            pl.BlockSpec((1, ATTN_WIDTH, tile), lambda b, i: (b, 0, i)),
            pl.BlockSpec((1, tile, POOL_WIDTH), lambda b, i: (b, i, 0)),
            pl.BlockSpec((1, HALO, POOL_WIDTH), lambda b, i: (b, jnp.maximum(i * hb - 1, 0), 0)),
            pl.BlockSpec((1, HALO, POOL_WIDTH),
                         lambda b, i: (b, jnp.minimum((i + 1) * hb, S // HALO - 1), 0)),
            const(wp.shape), const((1, POOL_WIDTH)),
            const((ATTN_WIDTH, D_MODEL)), const((POOL_WIDTH, D_MODEL)),
            const((1, D_MODEL)), const((1, D_MODEL)), const((1, D_MODEL)), const((1, D_MODEL)),
            const((N_EXPERTS, D_MODEL)),
        ],
        out_specs=[
            pl.BlockSpec((1, tile, D_MODEL), lambda b, i: (b, i, 0)),
            pl.BlockSpec((1, tile, D_MODEL), lambda b, i: (b, i, 0)),
            pl.BlockSpec((N_EXPERTS, tile), lambda b, i: (0, b * nt + i)),
        ],
        out_shape=[
            jax.ShapeDtypeStruct((B, S, D_MODEL), jnp.float32),
            jax.ShapeDtypeStruct((B, S, D_MODEL), jnp.bfloat16),
            jax.ShapeDtypeStruct((N_EXPERTS, B * S), jnp.float32),
        ],
        scratch_shapes=[
            pltpu.VMEM((tile + 2 * HALO + 8, POOL_WIDTH), jnp.float32),
            pltpu.VMEM((tile + 2 * HALO + 8, 2 * POOL_GROUP_W), jnp.float32),
            pltpu.VMEM((tile + 2 * HALO + 8, 2 * POOL_GROUP_W), jnp.float32),
            pltpu.VMEM((tile + 2 * HALO + 8, POOL_GROUP_W), jnp.float32),
        ],
        compiler_params=_cp(("parallel", "parallel")),
        name="mix_out",
    )(x, attn_t, u, u, u, wp, ps, woa, wop, lg, lb, g1, b1, wr_t)


def _route_kernel(aff_ref, tri_ref, ones_ref, low_ref, pos_ref, roff_ref, thr_ref, *, cap):
    n_e, R, _ = aff_ref.shape
    tri = tri_ref[...]
    ones = ones_ref[...]
    low = low_ref[...]

    def excl_prefix(flag):
        fb = flag.astype(jnp.bfloat16)
        incl = jnp.dot(fb, tri, preferred_element_type=jnp.float32)
        tot = jnp.dot(fb, ones, preferred_element_type=jnp.float32)
        before = jnp.dot(low, tot.astype(jnp.bfloat16), preferred_element_type=jnp.float32)
        return before + incl - flag, before

    for e in range(n_e):
        thr_ref[e] = 0

    def at_least(aff, bits):
        as_float = jnp.where(aff >= lax.bitcast_convert_type(bits, jnp.float32), 1.0, 0.0)
        as_bits = jnp.where(pltpu.bitcast(aff, jnp.int32) >= bits, 1.0, 0.0)
        return jnp.where(bits < F32_MIN_NORMAL_BITS, as_bits, as_float)

    def bit_step(it, carry):
        bit = 30 - it
        for e in range(n_e):
            cand = thr_ref[e] | lax.shift_left(jnp.int32(1), bit)
            cnt = jnp.sum(at_least(aff_ref[e], cand))
            thr_ref[e] = jnp.where(cnt >= float(cap), cand, thr_ref[e])
        return carry

    lax.fori_loop(0, 31, bit_step, 0)

    for e in range(n_e):
        aff = aff_ref[e]
        thr = thr_ref[e]
        gt = at_least(aff, thr + 1)
        eq = at_least(aff, thr) - gt
        need = float(cap) - jnp.sum(gt)
        eq_rank, _ = excl_prefix(eq)
        sel = gt + eq * jnp.where(eq_rank < need, 1.0, 0.0)
        pos, before = excl_prefix(sel)
        pos_ref[e] = jnp.where(sel > 0.5, pos, -1.0).astype(jnp.int32)
        roff_ref[e] = before.astype(jnp.int32)


def _route(aff3, cap):
    n_e, R, _ = aff3.shape
    tri = jnp.asarray(np.triu(np.ones((LANES, LANES), np.float32)), jnp.bfloat16)
    ones = jnp.ones((LANES, LANES), jnp.bfloat16)
    low = jnp.asarray(np.tril(np.ones((R, R), np.float32), -1), jnp.bfloat16)
    return pl.pallas_call(
        functools.partial(_route_kernel, cap=cap),
        out_shape=[jax.ShapeDtypeStruct((n_e, R, LANES), jnp.int32),
                   jax.ShapeDtypeStruct((n_e, R, LANES), jnp.int32)],
        scratch_shapes=[pltpu.SMEM((n_e,), jnp.int32)],
        compiler_params=pltpu.CompilerParams(vmem_limit_bytes=V7X_VMEM_LIMIT),
        name="route",
    )(aff3, tri, ones, low)


def _slot_base(off_ref, e, j, p, nt1):
    a = off_ref[e * nt1 + j]
    a16 = lax.shift_left(lax.shift_right_logical(a, 4), 4)
    return a16 + p * SLOT_WIN


def _window_onehot(pos_ref, rel_bases, min_pos, tile, value_rows=None):
    rows = lax.broadcasted_iota(jnp.int32, (SLOT_WIN, tile), 0)
    blocks = []
    for e in range(N_EXPERTS):
        pos = pos_ref[e:e + 1, :]
        rel = jnp.where(pos >= min_pos[e], pos - rel_bases[e], -1)
        hit = rows == rel
        if value_rows is None:
            blocks.append(jnp.where(hit, 1.0, 0.0).astype(jnp.bfloat16))
        else:
            blocks.append(jnp.where(hit, value_rows[e:e + 1, :], 0.0).astype(jnp.bfloat16))
    return jnp.concatenate(blocks, axis=0)


def _gather_kernel(off_ref, np_ref, pos_ref, h_ref, xe_ref, stage_ref, carry_ref, par_ref, sem,
                   *, nt, cap):
    j = pl.program_id(0)
    nt1 = nt + 1
    tile = h_ref.shape[0]

    def flush_copy(par, e, base):
        return pltpu.make_async_copy(stage_ref.at[par, e], xe_ref.at[e, pl.ds(base, SLOT_WIN)],
                                     sem.at[e])

    @pl.when(j == 0)
    def _():
        carry_ref[...] = jnp.zeros(carry_ref.shape, jnp.bfloat16)
        par_ref[0] = 0
        stage_ref[1] = jnp.zeros(stage_ref.shape[1:], jnp.bfloat16)
        for e in range(N_EXPERTS):
            flush_copy(1, e, cap).start()

    hb = h_ref[...]

    def one_pass(p, carry):
        par = par_ref[0]
        bases = [_slot_base(off_ref, e, j, p, nt1) for e in range(N_EXPERTS)]
        onehot = _window_onehot(pos_ref, bases, bases, tile)
        rows = jnp.dot(onehot, hb, preferred_element_type=jnp.float32).astype(jnp.bfloat16)
        stage_ref[par] = rows.reshape(N_EXPERTS, SLOT_WIN, D_MODEL)
        for e in range(N_EXPERTS):
            flush_copy(1 - par, e, 0).wait()
        for e in range(N_EXPERTS):
            b = off_ref[e * nt1 + j + 1]
            active = bases[e] <= b
            carry_in = jnp.where(p == 0, carry_ref[e], jnp.zeros_like(carry_ref[e]))
            stage_ref[par, e, 0:BF16_SUBLANES, :] = stage_ref[par, e, 0:BF16_SUBLANES, :] + carry_in
            dst = jnp.where(active, bases[e], cap)
            flush_copy(par, e, pl.multiple_of(dst, BF16_SUBLANES)).start()
            last = jnp.logical_and(active, b < bases[e] + SLOT_WIN)
            g = lax.shift_left(lax.shift_right_logical(b, 4), 4) - bases[e]
            g = jnp.clip(g, 0, SLOT_WIN - BF16_SUBLANES)
            tail = stage_ref[par, e, pl.ds(pl.multiple_of(g, BF16_SUBLANES), BF16_SUBLANES), :]
            carry_ref[e] = jnp.where(last, tail, carry_ref[e])
        par_ref[0] = 1 - par
        return carry

    lax.fori_loop(0, np_ref[j], one_pass, 0)

    @pl.when(j == nt - 1)
    def _():
        par = par_ref[0]
        for e in range(N_EXPERTS):
            flush_copy(1 - par, e, 0).wait()
        stage_ref[par] = jnp.zeros(stage_ref.shape[1:], jnp.bfloat16)
        for e in range(N_EXPERTS):
            flush_copy(par, e, cap).start()
        for e in range(N_EXPERTS):
            flush_copy(par, e, cap).wait()


def _gather(off, npass, pos, h, cap):
    N = h.shape[0]
    nt = N // TOK_TILE
    return pl.pallas_call(
        functools.partial(_gather_kernel, nt=nt, cap=cap),
        grid_spec=pltpu.PrefetchScalarGridSpec(
            num_scalar_prefetch=2,
            grid=(nt,),
            in_specs=[
                pl.BlockSpec((N_EXPERTS, TOK_TILE), lambda j, *_: (0, j)),
                pl.BlockSpec((TOK_TILE, D_MODEL), lambda j, *_: (j, 0)),
            ],
            out_specs=pl.BlockSpec(memory_space=pl.ANY),
            scratch_shapes=[
                pltpu.VMEM((2, N_EXPERTS, SLOT_WIN, D_MODEL), jnp.bfloat16),
                pltpu.VMEM((N_EXPERTS, BF16_SUBLANES, D_MODEL), jnp.bfloat16),
                pltpu.SMEM((1,), jnp.int32),
                pltpu.SemaphoreType.DMA((N_EXPERTS,)),
            ],
        ),
        out_shape=jax.ShapeDtypeStruct((N_EXPERTS, cap + SLOT_WIN, D_MODEL), jnp.bfloat16),
        compiler_params=_cp(("arbitrary",)),
        name="gather",
    )(off, npass, pos, h)


def _ffn_kernel(x_ref, wg_ref, wu_ref, wd_ref, y_ref):
    x = x_ref[0]
    half = D_FF // 2
    acc = None
    for c in range(2):
        cols = slice(c * half, (c + 1) * half)
        g = jnp.dot(x, wg_ref[0, :, cols], preferred_element_type=jnp.float32)
        u = jnp.dot(x, wu_ref[0, :, cols], preferred_element_type=jnp.float32)
        hid = (g * jax.nn.sigmoid(g) * u).astype(jnp.bfloat16)
        part = jnp.dot(hid, wd_ref[0, cols, :], preferred_element_type=jnp.float32)
        acc = part if acc is None else acc + part
    y_ref[0] = acc.astype(jnp.bfloat16)


def _ffn(xe, wg, wu, wd, cap, tile):
    n_e = xe.shape[0]
    return pl.pallas_call(
        _ffn_kernel,
        grid=(n_e, cap // tile),
        in_specs=[
            pl.BlockSpec((1, tile, D_MODEL), lambda e, s: (e, s, 0)),
            pl.BlockSpec((1, D_MODEL, D_FF), lambda e, s: (e, 0, 0)),
            pl.BlockSpec((1, D_MODEL, D_FF), lambda e, s: (e, 0, 0)),
            pl.BlockSpec((1, D_FF, D_MODEL), lambda e, s: (e, 0, 0)),
        ],
        out_specs=pl.BlockSpec((1, tile, D_MODEL), lambda e, s: (e, s, 0)),
        out_shape=jax.ShapeDtypeStruct((n_e, cap, D_MODEL), jnp.bfloat16),
        compiler_params=_cp(("parallel", "arbitrary")),
        name="ffn",
    )(xe, wg, wu, wd)


def _combine_kernel(off_ref, np_ref, pos_ref, aff_ref, h_ref, g2_ref, b2_ref, ye_ref, y_ref,
                    win_ref, slot_ref, sem, *, nt, cap):
    j = pl.program_id(0)
    nt1 = nt + 1
    tile = h_ref.shape[0]

    def fetch_copy(e, base, slot):
        return pltpu.make_async_copy(ye_ref.at[e, pl.ds(base, SLOT_WIN)], win_ref.at[slot, e],
                                     sem.at[slot])

    def read_base(e, jj, p):
        return jnp.minimum(_slot_base(off_ref, e, jj, p, nt1), cap - SLOT_WIN)

    def start_fetch(jj, p, slot):
        for e in range(N_EXPERTS):
            fetch_copy(e, pl.multiple_of(read_base(e, jj, p), BF16_SUBLANES), slot).start()

    def wait_fetch(slot):
        for e in range(N_EXPERTS):
            fetch_copy(e, 0, slot).wait()

    @pl.when(j == 0)
    def _():
        slot_ref[0] = 0
        start_fetch(0, 0, 0)

    n_pass = np_ref[j]
    gates = aff_ref[...]

    def one_pass(p):
        slot = slot_ref[0]
        wait_fetch(slot)

        @pl.when(p + 1 < n_pass)
        def _():
            start_fetch(j, p + 1, 1 - slot)

        @pl.when(jnp.logical_and(p + 1 >= n_pass, j + 1 < nt))
        def _():
            start_fetch(j + 1, 0, 1 - slot)

        onehot = _window_onehot(pos_ref,
                                [read_base(e, j, p) for e in range(N_EXPERTS)],
                                [_slot_base(off_ref, e, j, p, nt1) for e in range(N_EXPERTS)],
                                tile, value_rows=gates)
        rows = win_ref[slot].reshape(N_EXPERTS * SLOT_WIN, D_MODEL)
        slot_ref[0] = 1 - slot
        return lax.dot_general(onehot, rows, (((0,), (0,)), ((), ())),
                               preferred_element_type=jnp.float32)

    ff = lax.fori_loop(1, n_pass, lambda p, ff: ff + one_pass(p), one_pass(0))
    y_ref[...] = _ln(ALPHA * h_ref[...] + ff, g2_ref[...], b2_ref[...])


def _combine(off, npass, pos, aff, h, g2, b2, ye, cap):
    N = h.shape[0]
    nt = N // TOK_TILE
    return pl.pallas_call(
        functools.partial(_combine_kernel, nt=nt, cap=cap),
        grid_spec=pltpu.PrefetchScalarGridSpec(
            num_scalar_prefetch=2,
            grid=(nt,),
            in_specs=[
                pl.BlockSpec((N_EXPERTS, TOK_TILE), lambda j, *_: (0, j)),
                pl.BlockSpec((N_EXPERTS, TOK_TILE), lambda j, *_: (0, j)),
                pl.BlockSpec((TOK_TILE, D_MODEL), lambda j, *_: (j, 0)),
                pl.BlockSpec((1, D_MODEL), lambda j, *_: (0, 0)),
                pl.BlockSpec((1, D_MODEL), lambda j, *_: (0, 0)),
                pl.BlockSpec(memory_space=pl.ANY),
            ],
            out_specs=pl.BlockSpec((TOK_TILE, D_MODEL), lambda j, *_: (j, 0)),
            scratch_shapes=[
                pltpu.VMEM((2, N_EXPERTS, SLOT_WIN, D_MODEL), jnp.bfloat16),
                pltpu.SMEM((1,), jnp.int32),
                pltpu.SemaphoreType.DMA((2,)),
            ],
        ),
        out_shape=jax.ShapeDtypeStruct((N, D_MODEL), jnp.float32),
        compiler_params=_cp(("arbitrary",)),
        name="combine",
    )(off, npass, pos, aff, h, g2, b2, ye)


def _rope_tables(seq_len):
    rows = seq_len // GRID_W
    row = jnp.repeat(jnp.arange(rows), GRID_W).astype(jnp.float32)
    col = jnp.tile(jnp.arange(GRID_W), rows).astype(jnp.float32)
    inv_freq = ROPE_THETA ** (-jnp.arange(ROT_PAIRS, dtype=jnp.float32) / ROT_PAIRS)
    ang_r = inv_freq[:, None] * row[None, :]
    ang_c = inv_freq[:, None] * col[None, :]
    ct = jnp.concatenate([jnp.cos(ang_r), jnp.cos(ang_r), jnp.cos(ang_c), jnp.cos(ang_c)], axis=0)
    st = jnp.concatenate([-jnp.sin(ang_r), jnp.sin(ang_r), -jnp.sin(ang_c), jnp.sin(ang_c)], axis=0)
    return ct, st


def _trunk(x, prm, *, proj_tile, attn_tq, attn_tk, mix_tile, ffn_tile):
    B, S, _ = x.shape
    N = B * S
    cap = (CAPACITY_FACTOR * N) // N_EXPERTS
    assert S % proj_tile == 0 and S % attn_tq == 0 and S % attn_tk == 0 and S % mix_tile == 0
    assert N % TOK_TILE == 0 and TOK_TILE % LANES == 0
    assert cap % BF16_SUBLANES == 0 and cap >= SLOT_WIN and cap % ffn_tile == 0

    ct, st = _rope_tables(S)
    qt, k, vt, u = _in_proj(x, prm["ln_in_g"], prm["ln_in_b"], prm["wt"], prm["wu"], ct, st,
                            prm["gq"], prm["gk"], proj_tile)
    attn_t = _attention(qt, k, vt, attn_tq, attn_tk)
    h, hb, aff = _mix_out(x, attn_t, u, prm["w_pool"], prm["pool_scale"],
                      prm["wo_attn"], prm["wo_pool"], prm["ln_in_g"], prm["ln_in_b"],
                      prm["ln1_g"], prm["ln1_b"], prm["wr_t"], mix_tile)
    h = h.reshape(N, D_MODEL)

    pos3, roff3 = _route(aff.reshape(N_EXPERTS, N // LANES, LANES), cap)
    pos = pos3.reshape(N_EXPERTS, N)
    off = jnp.concatenate([roff3[:, ::TOK_TILE // LANES, 0],
                           jnp.full((N_EXPERTS, 1), cap, jnp.int32)], axis=1)
    a, b = off[:, :-1], off[:, 1:]
    a16 = (a // BF16_SUBLANES) * BF16_SUBLANES
    npass = jnp.max((b - a16) // SLOT_WIN + 1, axis=0).astype(jnp.int32)
    off = off.reshape(-1)

    xe = _gather(off, npass, pos, hb.reshape(N, D_MODEL), cap)
    ye = _ffn(xe, prm["w_gate"], prm["w_up"], prm["w_down"], cap, ffn_tile)
    y = _combine(off, npass, pos, aff, h, prm["ln2_g"], prm["ln2_b"], ye, cap)
    return y.reshape(B, S, D_MODEL)


def _prepare(ln_in_g, ln_in_b, w_in, q_norm_g, k_norm_g, w_pool, pool_scale, w_out, ln1_g, ln1_b,
             w_router, w_gate, w_up, w_down, ln2_g, ln2_b):
    bf = jnp.bfloat16
    row = lambda v: v.reshape(1, -1).astype(jnp.float32)
    qkv = ATTN_WIDTH + 2 * KV_WIDTH
    return {
        "ln_in_g": row(ln_in_g), "ln_in_b": row(ln_in_b),
        "wt": jnp.transpose(w_in[0][:, :qkv]).astype(bf),
        "wu": w_in[0][:, qkv:].astype(bf),
        "gq": q_norm_g[0].reshape(HEAD_DIM, 1), "gk": k_norm_g[0].reshape(HEAD_DIM, 1),
        "w_pool": w_pool[0].astype(bf), "pool_scale": row(pool_scale[0]),
        "wo_attn": w_out[0][:ATTN_WIDTH].astype(bf), "wo_pool": w_out[0][ATTN_WIDTH:].astype(bf),
        "ln1_g": row(ln1_g[0]), "ln1_b": row(ln1_b[0]),
        "wr_t": jnp.transpose(w_router[0]).astype(bf),
        "w_gate": w_gate[0].astype(bf), "w_up": w_up[0].astype(bf), "w_down": w_down[0].astype(bf),
        "ln2_g": row(ln2_g[0]), "ln2_b": row(ln2_b[0]),
    }


def kernel(x_prompt, x_sample, ln_in_g, ln_in_b, w_in, q_norm_g, k_norm_g, w_pool, pool_scale, w_out,
           ln1_g, ln1_b, w_router, w_gate, w_up, w_down, ln2_g, ln2_b):
    prm = _prepare(ln_in_g, ln_in_b, w_in, q_norm_g, k_norm_g, w_pool, pool_scale, w_out, ln1_g, ln1_b,
                   w_router, w_gate, w_up, w_down, ln2_g, ln2_b)
    tiles = dict(proj_tile=1024, attn_tq=512, attn_tk=512, mix_tile=512, ffn_tile=1024)
    return (_trunk(x_prompt, prm, **tiles), _trunk(x_sample, prm, **tiles))
```
